```python
import math
import jax, jax.numpy as jnp
from jax import lax
import numpy as np

D_MODEL = 1024
BATCH = 8
SEQ = 8192
DEPTH = 4

EPS = 1e-6
GLA_HEADS = 4
GLA_DK = 48
GLA_DV = 96
GLA_GATE_RANK = 16
GLA_GATE_NORM = 16.0
GLA_CHUNK = 64
DSA_HEADS = 6
DSA_DH = 64
IDX_HEADS = 4
IDX_DIM = 32
TOPK_MAX = 256
QUERY_BLOCK = 128
POOL_GROUPS = 4
POOL_GC = 64
POOL_WINDOWS = (2, 4, 8, 16)
REL_BUCKETS = 32
REL_MAX_DIST = 128

GLA_W = GLA_HEADS * GLA_DV
DSA_W = DSA_HEADS * DSA_DH
POOL_W = POOL_GROUPS * POOL_GC
D_MIX = GLA_W + DSA_W + POOL_W

IN_SIZES = (
    GLA_HEADS * GLA_DK,
    GLA_HEADS * GLA_DK,
    GLA_W,
    GLA_GATE_RANK,
    GLA_W,
    DSA_W,
    DSA_W,
    DSA_W,
    DSA_W,
    IDX_HEADS * IDX_DIM,
    IDX_DIM,
    IDX_HEADS,
    POOL_W,
    POOL_W,
)
IN_COLS = sum(IN_SIZES)

kernel_name = 'hybrid_gla_dsa_pool'


def split_cols(p):
    outs = []
    off = 0
    for s in IN_SIZES:
        outs.append(p[..., off:off + s])
        off += s
    return outs


def rmsnorm(x, g):
    xf = x.astype(jnp.float32)
    y = xf * lax.rsqrt(jnp.mean(xf * xf, axis=-1, keepdims=True) + EPS)
    return (y * g.astype(jnp.float32)).astype(x.dtype)


def t5_bucket(rel):
    rel = jnp.maximum(rel, 0)
    max_exact = REL_BUCKETS // 2
    relf = jnp.maximum(rel, 1).astype(jnp.float32)
    large = max_exact + (jnp.log(relf / max_exact) / math.log(REL_MAX_DIST / max_exact)
                         * (REL_BUCKETS - max_exact)).astype(jnp.int32)
    large = jnp.minimum(large, REL_BUCKETS - 1)
    return jnp.where(rel < max_exact, rel, large)


def gla_mixer(q, k, v, glog):
    B, T, H, dk = q.shape
    dv = v.shape[-1]
    N = T // GLA_CHUNK
    def chunk(a):
        return a.astype(jnp.float32).reshape(B, N, GLA_CHUNK, H, a.shape[-1])
    qf, kf, vf, gf = chunk(q), chunk(k), chunk(v), chunk(glog)
    b = jnp.cumsum(gf, axis=2)
    b_last = b[:, :, -1:]
    qe = qf * jnp.exp(b) * (dk ** -0.5)
    ke = kf * jnp.exp(-b)
    kd = kf * jnp.exp(b_last - b)
    A = jnp.einsum('bnihd,bnjhd->bnhij', qe, ke)
    tril = jnp.tril(jnp.ones((GLA_CHUNK, GLA_CHUNK), dtype=bool))
    A = jnp.where(tril, A, 0.0)
    o_intra = jnp.einsum('bnhij,bnjhv->bnihv', A, vf)
    U = jnp.einsum('bnjhd,bnjhv->bnhdv', kd, vf)
    decay = jnp.exp(b_last[:, :, 0])
    def step(S, inp):
        dec, u = inp
        return dec[..., None] * S + u, S
    S0 = jnp.zeros((B, H, dk, dv), jnp.float32)
    _, S_prev = lax.scan(step, S0, (jnp.moveaxis(decay, 1, 0), jnp.moveaxis(U, 1, 0)))
    S_prev = jnp.moveaxis(S_prev, 0, 1)
    o_inter = jnp.einsum('bnihd,bnhdv->bnihv', qe, S_prev)
    return (o_intra + o_inter).reshape(B, T, H, dv)


def dsa_mixer(q, k, v, q_idx, k_idx, w_idx, rel_bias):
    B, T, H, dh = q.shape
    topk = min(TOPK_MAX, T // 4)
    NB = T // QUERY_BLOCK
    def blk(a):
        return jnp.moveaxis(a.reshape(B, NB, QUERY_BLOCK, *a.shape[2:]), 1, 0)
    pos_blk = jnp.arange(T, dtype=jnp.int32).reshape(NB, QUERY_BLOCK)
    key_pos = jnp.arange(T, dtype=jnp.int32)
    kidx_f = k_idx.astype(jnp.float32)
    rb = rel_bias.astype(jnp.float32)

    def body(args):
        qb, qib, wb, tq = args
        s = jax.nn.relu(jnp.einsum('bqhd,bsd->bqhs', qib.astype(jnp.float32), kidx_f)
                        * (IDX_DIM ** -0.5))
        score = jnp.einsum('bqhs,bqh->bqs', s, wb.astype(jnp.float32) * (IDX_HEADS ** -0.5))
        visible = key_pos[None, :] <= tq[:, None]
        score = jnp.where(visible[None], score, -jnp.inf)
        _, idx = lax.top_k(score, topk)
        kg = jax.vmap(lambda kk, ii: kk[ii])(k, idx)
        vg = jax.vmap(lambda vv, ii: vv[ii])(v, idx)
        logits = jnp.einsum('bqhd,bqkhd->bhqk', qb.astype(jnp.float32),
                            kg.astype(jnp.float32)) * (dh ** -0.5)
        rel = tq[None, :, None] - idx
        bias = rb[t5_bucket(rel)]
        logits = logits + jnp.moveaxis(bias, -1, 1)
        logits = jnp.where((rel >= 0)[:, None], logits, -1e30)
        p = jax.nn.softmax(logits, axis=-1)
        o = jnp.einsum('bhqk,bqkhd->bqhd', p, vg.astype(jnp.float32))
        return o.astype(q.dtype)

    o = lax.map(body, (blk(q), blk(q_idx), blk(w_idx), pos_blk))
    return jnp.moveaxis(o, 0, 1).reshape(B, T, H * dh)


def pool_mixer(u, pool_w, pool_scale):
    B, T, _ = u.shape
    ug = u.astype(jnp.float32).reshape(B, T, POOL_GROUPS, POOL_GC)
    cs = jnp.cumsum(ug, axis=1)
    pos = jnp.arange(T, dtype=jnp.int32)
    outs = []
    for gi, w in enumerate(POOL_WINDOWS):
        c = cs[:, :, gi]
        lower = jnp.pad(c, ((0, 0), (w, 0), (0, 0)))[:, :T]
        cnt = jnp.minimum(pos + 1, w).astype(jnp.float32)[None, :, None]
        outs.append((c - lower) / cnt - ug[:, :, gi])
    pooled = jnp.stack(outs, axis=2)
    y = jnp.einsum('btgc,gcd->btgd', pooled, pool_w.astype(jnp.float32)).reshape(B, T, POOL_W)
    return (y * pool_scale.astype(jnp.float32)).astype(u.dtype)


def setup_inputs(seed: int = 0) -> dict:
    key = jax.random.key(seed)
    ks = jax.random.split(key, 12)
    f = jnp.float32
    x = jax.random.normal(ks[0], (BATCH, SEQ, D_MODEL), f)
    norm_g = 1.0 + 0.05 * jax.random.normal(ks[1], (DEPTH, D_MODEL), f)
    w_in = jax.random.normal(ks[2], (DEPTH, D_MODEL, IN_COLS), f) * D_MODEL ** -0.5
    gla_gate_w2 = jax.random.normal(ks[3], (DEPTH, GLA_GATE_RANK, GLA_HEADS * GLA_DK), f) * GLA_GATE_RANK ** -0.5
    gla_gate_b = 0.1 * jax.random.normal(ks[4], (DEPTH, GLA_HEADS * GLA_DK), f)
    gla_norm_g = 1.0 + 0.05 * jax.random.normal(ks[5], (DEPTH, GLA_DV), f)
    rel_bias = 0.1 * jax.random.normal(ks[6], (REL_BUCKETS, DSA_HEADS), f)
    pool_w = jax.random.normal(ks[7], (DEPTH, POOL_GROUPS, POOL_GC, POOL_GC), f) * POOL_GC ** -0.5
    pool_scale = 1.0 + 0.1 * jax.random.normal(ks[8], (DEPTH, POOL_W), f)
    w_out = jax.random.normal(ks[9], (DEPTH, D_MIX, D_MODEL), f) * D_MIX ** -0.5
    final_norm_g = 1.0 + 0.05 * jax.random.normal(ks[10], (D_MODEL,), f)
    return {'x': x, 'norm_g': norm_g, 'w_in': w_in, 'gla_gate_w2': gla_gate_w2,
            'gla_gate_b': gla_gate_b, 'gla_norm_g': gla_norm_g, 'rel_bias': rel_bias,
            'pool_w': pool_w, 'pool_scale': pool_scale, 'w_out': w_out,
            'final_norm_g': final_norm_g}


def reference(x, norm_g, w_in, gla_gate_w2, gla_gate_b, gla_norm_g, rel_bias,
              pool_w, pool_scale, w_out, final_norm_g):
    B, T, _ = x.shape
    for l in range(DEPTH):
        h = rmsnorm(x, norm_g[l])
        p = jnp.einsum('btd,dc->btc', h, w_in[l])
        (gq, gk, gv, gz, ggate, dq, dk, dv, dgate,
         qi, ki, wi, pu, pgate) = split_cols(p)
        glog = jax.nn.log_sigmoid((jnp.einsum('btr,rc->btc', gz, gla_gate_w2[l])
                                   + gla_gate_b[l]).astype(jnp.float32)) / GLA_GATE_NORM
        o_gla = gla_mixer(gq.reshape(B, T, GLA_HEADS, GLA_DK),
                          gk.reshape(B, T, GLA_HEADS, GLA_DK),
                          gv.reshape(B, T, GLA_HEADS, GLA_DV),
                          glog.reshape(B, T, GLA_HEADS, GLA_DK))
        o_gla = rmsnorm(o_gla, gla_norm_g[l]).reshape(B, T, GLA_W).astype(x.dtype)
        y_gla = o_gla * jax.nn.silu(ggate)
        o_dsa = dsa_mixer(dq.reshape(B, T, DSA_HEADS, DSA_DH),
                          dk.reshape(B, T, DSA_HEADS, DSA_DH),
                          dv.reshape(B, T, DSA_HEADS, DSA_DH),
                          qi.reshape(B, T, IDX_HEADS, IDX_DIM), ki, wi, rel_bias)
        y_dsa = o_dsa * jax.nn.silu(dgate)
        y_pool = pool_mixer(pu, pool_w[l], pool_scale[l]) * jax.nn.silu(pgate)
        y = jnp.concatenate([y_gla, y_dsa, y_pool], axis=-1)
        x = x + jnp.einsum('btc,cd->btd', y, w_out[l])
    return rmsnorm(x, final_norm_g)
```

```python
import functools
import math

import numpy as np
import jax
import jax.numpy as jnp
from jax import lax
from jax.experimental import pallas as pl
from jax.experimental.pallas import tpu as pltpu

D_MODEL = 1024
DEPTH = 4
EPS = 1e-6
GLA_HEADS = 4
GLA_DK = 48
GLA_DV = 96
GLA_GATE_RANK = 16
GLA_GATE_NORM = 16.0
GLA_CHUNK = 64
DSA_HEADS = 6
DSA_DH = 64
IDX_HEADS = 4
IDX_DIM = 32
TOPK_MAX = 256
POOL_GROUPS = 4
POOL_GC = 64
POOL_WINDOWS = (2, 4, 8, 16)
REL_BUCKETS = 32
REL_MAX_DIST = 128

GLA_W = GLA_HEADS * GLA_DV
DSA_W = DSA_HEADS * DSA_DH
POOL_W = POOL_GROUPS * POOL_GC
IN_SIZES = (GLA_HEADS * GLA_DK, GLA_HEADS * GLA_DK, GLA_W, GLA_GATE_RANK, GLA_W,
            DSA_W, DSA_W, DSA_W, DSA_W, IDX_HEADS * IDX_DIM, IDX_DIM, IDX_HEADS,
            POOL_W, POOL_W)

LANE = 128
SUBLANE = 8
VMEM_LIMIT_BYTES = 56 * 1024 * 1024

GLA_HP = GLA_HEADS * LANE
POOL_HALO = 16

_NAT = {}
_off = 0
for _name, _w in (("gq", GLA_HP), ("gk", GLA_HP), ("gv", GLA_HP), ("gg", GLA_HP), ("gz", LANE),
                  ("dk", DSA_W), ("kx", LANE), ("dg", DSA_W), ("pu", POOL_W), ("pg", POOL_W)):
    _NAT[_name] = (_off, _off + _w)
    _off += _w
NAT_COLS = _off
_TR = {}
_off = 0
for _name, _w in (("dq", DSA_W), ("dv", DSA_W), ("qi", IDX_HEADS * IDX_DIM), ("wi", SUBLANE)):
    _TR[_name] = (_off, _off + _w)
    _off += _w
TR_ROWS = _off

INT_MIN = -2 ** 31
NEG_BIG = -1e30

f32 = jnp.float32
bf16 = jnp.bfloat16
i32 = jnp.int32


def _silu(x):
    return x * jax.nn.sigmoid(x)


def _dot(a, b):
    return jnp.dot(a, b, preferred_element_type=f32)


def _dot_nt(a, b):
    return lax.dot_general(a, b, (((1,), (1,)), ((), ())), preferred_element_type=f32)


def _dot_tn(a, b):
    return lax.dot_general(a, b, (((0,), (0,)), ((), ())), preferred_element_type=f32)


def _params():
    return pltpu.CompilerParams(dimension_semantics=("arbitrary", "arbitrary"),
                                vmem_limit_bytes=VMEM_LIMIT_BYTES)


def _in_proj_kernel(x_ref, g_ref, wn_ref, wt_ref,
                    gq_ref, gk_ref, gv_ref, gg_ref, gz_ref, dk_ref, kx_ref, dg_ref, pu_ref, pg_ref,
                    dqT_ref, dvT_ref, qiT_ref, wiT_ref):
    x = x_ref[...]
    h = x * lax.rsqrt(jnp.mean(x * x, axis=-1, keepdims=True) + EPS) * g_ref[...]
    hb = h.astype(bf16)
    for name, ref in (("gq", gq_ref), ("gk", gk_ref), ("gv", gv_ref), ("gg", gg_ref), ("gz", gz_ref),
                      ("dk", dk_ref), ("kx", kx_ref), ("dg", dg_ref), ("pu", pu_ref), ("pg", pg_ref)):
        lo, hi = _NAT[name]
        ref[...] = _dot(hb, wn_ref[:, lo:hi]).astype(ref.dtype)
    for name, ref in (("dq", dqT_ref), ("dv", dvT_ref), ("qi", qiT_ref), ("wi", wiT_ref)):
        lo, hi = _TR[name]
        ref[...] = _dot_nt(wt_ref[lo:hi, :], hb).astype(ref.dtype)


def _in_proj(x, g, wn, wt, tm):
    B, T, D = x.shape
    nat_dtypes = {"gq": f32, "gk": f32, "gv": f32, "gg": f32, "gz": f32,
                  "dk": bf16, "kx": bf16, "dg": f32, "pu": f32, "pg": f32}
    tr_dtypes = {"dq": bf16, "dv": bf16, "qi": bf16, "wi": f32}
    out_shape, out_specs = [], []
    for name, dt in nat_dtypes.items():
        w = _NAT[name][1] - _NAT[name][0]
        out_shape.append(jax.ShapeDtypeStruct((B, T, w), dt))
        out_specs.append(pl.BlockSpec((None, tm, w), lambda b, t: (b, t, 0)))
    for name, dt in tr_dtypes.items():
        w = _TR[name][1] - _TR[name][0]
        out_shape.append(jax.ShapeDtypeStruct((B, w, T), dt))
        out_specs.append(pl.BlockSpec((None, w, tm), lambda b, t: (b, 0, t)))
    return pl.pallas_call(
        _in_proj_kernel,
        grid=(B, T // tm),
        in_specs=[pl.BlockSpec((None, tm, D), lambda b, t: (b, t, 0)),
                  pl.BlockSpec((1, D), lambda b, t: (0, 0)),
                  pl.BlockSpec((D, NAT_COLS), lambda b, t: (0, 0)),
                  pl.BlockSpec((TR_ROWS, D), lambda b, t: (0, 0))],
        out_specs=out_specs,
        out_shape=out_shape,
        compiler_params=_params(),
        name="in_proj",
    )(x, g, wn, wt)


def _gla_kernel(gq_ref, gk_ref, gv_ref, gg_ref, gz_ref, w2_ref, b2_ref, gn_ref, out_ref,
                st_ref, glog_ref):
    C = GLA_CHUNK
    n_chunks = gq_ref.shape[0] // C

    @pl.when(pl.program_id(1) == 0)
    def _():
        st_ref[...] = jnp.zeros_like(st_ref)

    z = _dot(gz_ref[...].astype(bf16), w2_ref[...]) + b2_ref[...]
    glog_ref[...] = (jnp.minimum(z, 0.0) - jnp.log1p(jnp.exp(-jnp.abs(z)))) / GLA_GATE_NORM

    row = lax.broadcasted_iota(i32, (C, C), 0)
    col = lax.broadcasted_iota(i32, (C, C), 1)
    causal = col <= row
    tril = causal.astype(f32)
    gn = gn_ref[...]

    def chunk(c, carry):
        r = pl.ds(pl.multiple_of(c * C, C), C)
        for h in range(GLA_HEADS):
            cs = slice(h * LANE, (h + 1) * LANE)
            b = jnp.dot(tril, glog_ref[r, cs], precision=lax.Precision.HIGHEST,
                        preferred_element_type=f32)
            b_last = b[C - 1:C, :]
            q = gq_ref[r, cs]
            k = gk_ref[r, cs]
            v = gv_ref[r, cs].astype(bf16)
            qe = (q * jnp.exp(b) * (GLA_DK ** -0.5)).astype(bf16)
            ke = (k * jnp.exp(-b)).astype(bf16)
            kd = (k * jnp.exp(b_last - b)).astype(bf16)
            a = jnp.where(causal, _dot_nt(qe, ke), 0.0)
            st = st_ref[h]
            o = _dot(a.astype(bf16), v) + _dot_nt(qe, st.astype(bf16))
            st_ref[h] = st * jnp.exp(b_last) + _dot_tn(v, kd)
            ms = jnp.sum(o * o, axis=-1, keepdims=True) * (1.0 / GLA_DV)
            y = o * lax.rsqrt(ms + EPS) * gn * _silu(gg_ref[r, cs])
            out_ref[r, cs] = y.astype(out_ref.dtype)
        return carry

    lax.fori_loop(0, n_chunks, chunk, 0)


def _gla(gq, gk, gv, gg, gz, w2, b2, gn, tg):
    B, T, _ = gq.shape
    tok = lambda w: pl.BlockSpec((None, tg, w), lambda b, t: (b, t, 0))
    const = lambda s: pl.BlockSpec(s, lambda b, t: (0, 0))
    return pl.pallas_call(
        _gla_kernel,
        grid=(B, T // tg),
        in_specs=[tok(GLA_HP), tok(GLA_HP), tok(GLA_HP), tok(GLA_HP), tok(LANE),
                  const((LANE, GLA_HP)), const((1, GLA_HP)), const((1, LANE))],
        out_specs=tok(GLA_HP),
        out_shape=jax.ShapeDtypeStruct((B, T, GLA_HP), bf16),
        scratch_shapes=[pltpu.VMEM((GLA_HEADS, LANE, LANE), f32),
                        pltpu.VMEM((tg, GLA_HP), f32)],
        compiler_params=_params(),
        name="gla",
    )(gq, gk, gv, gg, gz, w2, b2, gn)


def _pool_kernel(pu_ref, pg_ref, w_ref, sc_ref, out_ref, halo_ref):
    tp = pu_ref.shape[0]
    H = POOL_HALO

    @pl.when(pl.program_id(1) == 0)
    def _():
        halo_ref[...] = jnp.zeros_like(halo_ref)

    u = pu_ref[...]
    ext = jnp.concatenate([halo_ref[...], u], axis=0)
    halo_ref[...] = u[tp - H:, :]
    s2 = ext + pltpu.roll(ext, 1, 0)
    s4 = s2 + pltpu.roll(s2, 2, 0)
    s8 = s4 + pltpu.roll(s4, 4, 0)
    s16 = s8 + pltpu.roll(s8, 8, 0)
    lane = lax.broadcasted_iota(i32, (tp, POOL_W), 1)
    grp = lane // POOL_GC
    sw = jnp.where(grp == 0, s2[H:], jnp.where(grp == 1, s4[H:], jnp.where(grp == 2, s8[H:], s16[H:])))
    win = jnp.where(grp == 0, POOL_WINDOWS[0],
                    jnp.where(grp == 1, POOL_WINDOWS[1],
                              jnp.where(grp == 2, POOL_WINDOWS[2], POOL_WINDOWS[3])))
    t = pl.program_id(1) * tp + lax.broadcasted_iota(i32, (tp, POOL_W), 0)
    cnt = jnp.minimum(t + 1, win).astype(f32)
    pooled = sw / cnt - u
    y = _dot(pooled.astype(bf16), w_ref[...]) * sc_ref[...]
    out_ref[...] = (y * _silu(pg_ref[...])).astype(out_ref.dtype)


def _pool(pu, pg, wbd, sc, tp):
    B, T, _ = pu.shape
    tok = pl.BlockSpec((None, tp, POOL_W), lambda b, t: (b, t, 0))
    return pl.pallas_call(
        _pool_kernel,
        grid=(B, T // tp),
        in_specs=[tok, tok,
                  pl.BlockSpec((POOL_W, POOL_W), lambda b, t: (0, 0)),
                  pl.BlockSpec((1, POOL_W), lambda b, t: (0, 0))],
        out_specs=tok,
        out_shape=jax.ShapeDtypeStruct((B, T, POOL_W), bf16),
        scratch_shapes=[pltpu.VMEM((POOL_HALO, POOL_W), f32)],
        compiler_params=_params(),
        name="pool",
    )(pu, pg, wbd, sc)


def _dsa_kernel(qT_ref, qiT_ref, wiT_ref, k_ref, kx_ref, vT_ref, dg_ref, bias_ref, out_ref,
                keys_ref, qm_ref, qim_ref, accT_ref, *, topk):
    tq = qT_ref.shape[1]
    tk = tq
    qb = pl.program_id(1)
    n_blocks = qb + 1

    def rows(kb):
        return pl.ds(pl.multiple_of(kb * tk, tk), tk)

    prow = lax.broadcasted_iota(i32, (LANE, tq), 0)
    for h in range(DSA_HEADS):
        pair = qT_ref[(h // 2) * LANE:(h // 2 + 1) * LANE, :]
        qm_ref[h] = jnp.where((prow // DSA_DH) == (h % 2), pair, jnp.zeros_like(pair))
    qi = qiT_ref[...]
    for h in range(IDX_HEADS):
        qim_ref[h] = jnp.where((prow // IDX_DIM) == h, qi, jnp.zeros_like(qi))
    w = wiT_ref[...] * ((IDX_DIM ** -0.5) * (IDX_HEADS ** -0.5))

    def score_keys(kb):
        kx = kx_ref[rows(kb), :]
        s = jnp.zeros((tk, tq), f32)
        for h in range(IDX_HEADS):
            s = s + jnp.maximum(_dot(kx, qim_ref[h]), 0.0) * w[h:h + 1, :]
        bits = lax.bitcast_convert_type(s, i32)
        return jnp.where(bits < 0, INT_MIN - bits, bits)

    def score_body(kb, c):
        keys_ref[rows(kb), :] = score_keys(kb)
        return c

    lax.fori_loop(0, qb, score_body, 0)
    krow = lax.broadcasted_iota(i32, (tk, tq), 0)
    qcol = lax.broadcasted_iota(i32, (tk, tq), 1)
    keys_ref[rows(qb), :] = jnp.where(krow <= qcol, score_keys(qb), INT_MIN)

    def count(pred):
        def body(kb, acc):
            c = pred(keys_ref[rows(kb), :]).astype(i32)
            return acc + jnp.sum(c.reshape(tk // SUBLANE, SUBLANE, tq), axis=0)
        acc = lax.fori_loop(0, n_blocks, body, jnp.zeros((SUBLANE, tq), i32))
        return jnp.sum(acc, axis=0, keepdims=True)

    def radix_step(i, prefix):
        cand = prefix + lax.shift_left(jnp.int32(1), 31 - i)
        return jnp.where(count(lambda blk: blk >= cand) >= topk, cand, prefix)

    tau = lax.fori_loop(0, 32, radix_step, jnp.full((1, tq), INT_MIN, i32))
    n_gt = count(lambda blk: blk > tau)
    need = jnp.where(tau == INT_MIN, 0, topk - n_gt).astype(f32)

    tri = (lax.broadcasted_iota(i32, (tk, tk), 1) <= lax.broadcasted_iota(i32, (tk, tk), 0)).astype(bf16)

    def mask_body(kb, run):
        keys = keys_ref[rows(kb), :]
        eq = keys == tau
        pre = _dot(tri, eq.astype(bf16)) + run
        sel = (keys > tau) | (eq & (pre <= need))
        keys_ref[rows(kb), :] = lax.bitcast_convert_type(jnp.where(sel, 0.0, NEG_BIG).astype(f32), i32)
        return pre[tk - 1:tk, :]

    lax.fori_loop(0, n_blocks, mask_body, jnp.zeros((1, tq), f32))

    n_far = jnp.maximum(qb - 1, 0)
    for h in range(DSA_HEADS):
        qm = qm_ref[h]
        kcols = slice((h // 2) * LANE, (h // 2 + 1) * LANE)
        vrows = slice(h * DSA_DH, (h + 1) * DSA_DH)

        def attend(kb, carry, bias):
            m, l, acc = carry
            lg = _dot(k_ref[rows(kb), kcols], qm) + lax.bitcast_convert_type(keys_ref[rows(kb), :], f32)
            if bias is not None:
                lg = lg + bias
            m_new = jnp.maximum(m, jnp.max(lg, axis=0, keepdims=True))
            alpha = jnp.exp(m - m_new)
            p = jnp.exp(lg - m_new)
            l = alpha * l + jnp.sum(p, axis=0, keepdims=True)
            acc = alpha * acc + _dot(vT_ref[vrows, rows(kb)], p.astype(bf16))
            return m_new, l, acc

        def far_body(kb, carry):
            return attend(kb, carry, None)

        def near_body(kb, carry):
            j = kb - (qb - 1)
            return attend(kb, carry, bias_ref[h, pl.ds(pl.multiple_of(j * tk, tk), tk), :])

        init = (jnp.full((1, tq), NEG_BIG, f32), jnp.zeros((1, tq), f32), jnp.zeros((DSA_DH, tq), f32))
        carry = lax.fori_loop(0, n_far, far_body, init)
        m, l, acc = lax.fori_loop(n_far, n_blocks, near_body, carry)
        accT_ref[vrows, :] = acc / l

    out_ref[...] = (accT_ref[...].T * _silu(dg_ref[...])).astype(out_ref.dtype)


def _dsa(dqT, qiT, wiT, dk, kx, dvT, dg, bias, tq, topk):
    B, T, _ = dk.shape
    whole = lambda shape, imap: pl.BlockSpec(shape, imap, pipeline_mode=pl.Buffered(1))
    return pl.pallas_call(
        functools.partial(_dsa_kernel, topk=topk),
        grid=(B, T // tq),
        in_specs=[pl.BlockSpec((None, DSA_W, tq), lambda b, q: (b, 0, q)),
                  pl.BlockSpec((None, IDX_HEADS * IDX_DIM, tq), lambda b, q: (b, 0, q)),
                  pl.BlockSpec((None, SUBLANE, tq), lambda b, q: (b, 0, q)),
                  whole((None, T, DSA_W), lambda b, q: (b, 0, 0)),
                  whole((None, T, LANE), lambda b, q: (b, 0, 0)),
                  whole((None, DSA_W, T), lambda b, q: (b, 0, 0)),
                  pl.BlockSpec((None, tq, DSA_W), lambda b, q: (b, q, 0)),
                  whole((DSA_HEADS, 2 * tq, tq), lambda b, q: (0, 0, 0))],
        out_specs=pl.BlockSpec((None, tq, DSA_W), lambda b, q: (b, q, 0)),
        out_shape=jax.ShapeDtypeStruct((B, T, DSA_W), bf16),
        scratch_shapes=[pltpu.VMEM((T, tq), i32),
                        pltpu.VMEM((DSA_HEADS, LANE, tq), bf16),
                        pltpu.VMEM((IDX_HEADS, LANE, tq), bf16),
                        pltpu.VMEM((DSA_W, tq), f32)],
        compiler_params=_params(),
        name="dsa",
    )(dqT, qiT, wiT, dk, kx, dvT, dg, bias)


def _out_proj_kernel(x_ref, yg_ref, yd_ref, yp_ref, wg_ref, wd_ref, wp_ref, fg_ref, out_ref, *, final):
    x = x_ref[...] + _dot(yg_ref[...], wg_ref[...]) + _dot(yd_ref[...], wd_ref[...]) \
        + _dot(yp_ref[...], wp_ref[...])
    if final:
        x = x * lax.rsqrt(jnp.mean(x * x, axis=-1, keepdims=True) + EPS) * fg_ref[...]
    out_ref[...] = x


def _out_proj(x, yg, yd, yp, wg, wd, wp, fg, tm, final):
    B, T, D = x.shape
    tok = lambda w: pl.BlockSpec((None, tm, w), lambda b, t: (b, t, 0))
    const = lambda s: pl.BlockSpec(s, lambda b, t: (0, 0))
    return pl.pallas_call(
        functools.partial(_out_proj_kernel, final=final),
        grid=(B, T // tm),
        in_specs=[tok(D), tok(GLA_HP), tok(DSA_W), tok(POOL_W),
                  const((GLA_HP, D)), const((DSA_W, D)), const((POOL_W, D)), const((1, D))],
        out_specs=tok(D),
        out_shape=jax.ShapeDtypeStruct((B, T, D), f32),
        compiler_params=_params(),
        name="out_proj_final" if final else "out_proj",
    )(x, yg, yd, yp, wg, wd, wp, fg)


def _pad_heads(w, heads, d):
    lead = w.shape[:-1]
    w = w.reshape(*lead, heads, d)
    w = jnp.pad(w, [(0, 0)] * len(lead) + [(0, 0), (0, LANE - d)])
    return w.reshape(*lead, heads * LANE)


def _split_cols(w):
    outs, off = [], 0
    for s in IN_SIZES:
        outs.append(w[..., off:off + s])
        off += s
    return outs


def _layer_weights(w_in, gate_w2, gate_b, gla_norm_g, pool_w, pool_scale, w_out):
    (gq, gk, gv, gz, gg, dq, dk, dv, dg, qi, ki, wi, pu, pg) = _split_cols(w_in)
    D = w_in.shape[0]
    wn = jnp.concatenate([
        _pad_heads(gq, GLA_HEADS, GLA_DK), _pad_heads(gk, GLA_HEADS, GLA_DK),
        _pad_heads(gv, GLA_HEADS, GLA_DV), _pad_heads(gg, GLA_HEADS, GLA_DV),
        jnp.pad(gz, ((0, 0), (0, LANE - GLA_GATE_RANK))),
        dk, jnp.tile(ki, (1, IDX_HEADS)), dg, pu, pg], axis=1).astype(bf16)
    wt = jnp.concatenate([
        dq * (DSA_DH ** -0.5), dv, qi, jnp.pad(wi, ((0, 0), (0, SUBLANE - IDX_HEADS)))], axis=1).T.astype(bf16)
    w2 = jnp.pad(_pad_heads(gate_w2, GLA_HEADS, GLA_DK), ((0, LANE - GLA_GATE_RANK), (0, 0))).astype(bf16)
    b2 = _pad_heads(gate_b[None, :], GLA_HEADS, GLA_DK)
    gn = jnp.pad(gla_norm_g[None, :], ((0, 0), (0, LANE - GLA_DV)))
    wbd = jax.scipy.linalg.block_diag(*[pool_w[g] for g in range(POOL_GROUPS)]).astype(bf16)
    sc = pool_scale[None, :]
    wg = _pad_heads(w_out[:GLA_W].T, GLA_HEADS, GLA_DV).T.astype(bf16)
    wd = w_out[GLA_W:GLA_W + DSA_W].astype(bf16)
    wp = w_out[GLA_W + DSA_W:].astype(bf16)
    return wn, wt, w2, b2, gn, wbd, sc, wg, wd, wp


def _t5_bucket_table(n):
    rel = np.arange(n)
    max_exact = REL_BUCKETS // 2
    relf = np.maximum(rel, 1).astype(np.float32)
    large = max_exact + (np.log(relf / np.float32(max_exact)) / np.float32(math.log(REL_MAX_DIST / max_exact))
                         * np.float32(REL_BUCKETS - max_exact)).astype(np.int32)
    large = np.minimum(large, REL_BUCKETS - 1)
    return np.where(rel < max_exact, rel, large)


def _near_bias(rel_bias, tq):
    bucket = _t5_bucket_table(2 * tq)
    assert np.all(bucket[tq + 1:] == REL_BUCKETS - 1)
    j = np.arange(2 * tq)[:, None]
    i = np.arange(tq)[None, :]
    rel = np.maximum(tq + i - j, 0)
    by_rel = rel_bias[bucket] - rel_bias[REL_BUCKETS - 1][None, :]
    return jnp.transpose(by_rel[rel], (2, 0, 1))


def kernel(x, norm_g, w_in, gla_gate_w2, gla_gate_b, gla_norm_g, rel_bias, pool_w, pool_scale, w_out,
           final_norm_g):
    B, T, D = x.shape
    assert D == D_MODEL
    tm = min(512, T)
    tq = min(256, T)
    topk = min(TOPK_MAX, T // 4)
    assert T % tm == 0 and T % tq == 0 and tq >= LANE and T % GLA_CHUNK == 0
    bias = _near_bias(rel_bias.astype(f32), tq)
    fg = final_norm_g[None, :]
    for l in range(DEPTH):
        wn, wt, w2, b2, gn, wbd, sc, wg, wd, wp = _layer_weights(
            w_in[l], gla_gate_w2[l], gla_gate_b[l], gla_norm_g[l], pool_w[l], pool_scale[l], w_out[l])
        (gq, gk, gv, gg, gz, dk, kx, dg, pu, pg, dqT, dvT, qiT, wiT) = _in_proj(x, norm_g[l][None, :], wn, wt, tm)
        yg = _gla(gq, gk, gv, gg, gz, w2, b2, gn, tm)
        yp = _pool(pu, pg, wbd, sc, tm)
        yd = _dsa(dqT, qiT, wiT, dk, kx, dvT, dg, bias, tq, topk)
        x = _out_proj(x, yg, yd, yp, wg, wd, wp, fg, tm, final=(l == DEPTH - 1))
    return x
```

```python
import functools
import math

import numpy as np
import jax
import jax.numpy as jnp
from jax import lax
from jax.experimental import pallas as pl
from jax.experimental.pallas import tpu as pltpu

D_MODEL = 1024
DEPTH = 4
EPS = 1e-6
GLA_HEADS = 4
GLA_DK = 48
GLA_DV = 96
GLA_GATE_RANK = 16
GLA_GATE_NORM = 16.0
GLA_CHUNK = 64
DSA_HEADS = 6
DSA_DH = 64
IDX_HEADS = 4
IDX_DIM = 32
TOPK_MAX = 256
POOL_GROUPS = 4
POOL_GC = 64
POOL_WINDOWS = (2, 4, 8, 16)
REL_BUCKETS = 32
REL_MAX_DIST = 128

GLA_W = GLA_HEADS * GLA_DV
DSA_W = DSA_HEADS * DSA_DH
POOL_W = POOL_GROUPS * POOL_GC
IN_SIZES = (GLA_HEADS * GLA_DK, GLA_HEADS * GLA_DK, GLA_W, GLA_GATE_RANK, GLA_W,
            DSA_W, DSA_W, DSA_W, DSA_W, IDX_HEADS * IDX_DIM, IDX_DIM, IDX_HEADS,
            POOL_W, POOL_W)

LANE = 128
SUBLANE = 8
VMEM_LIMIT_BYTES = 56 * 1024 * 1024

GLA_HP = GLA_HEADS * LANE
POOL_HALO = 16
ONES_ROWS = 16
LOG2E = math.log2(math.e)
DSA_Q_SCALE = (DSA_DH ** -0.5) * LOG2E

_NAT = {}
_off = 0
for _name, _w in (("gq", GLA_HP), ("gk", GLA_HP), ("gv", GLA_HP), ("gg", GLA_HP), ("gz", LANE),
                  ("dk", DSA_W), ("kx", LANE), ("dg", DSA_W), ("pu", POOL_W), ("pg", POOL_W)):
    _NAT[_name] = (_off, _off + _w)
    _off += _w
NAT_COLS = _off
_TR = {}
_off = 0
for _name, _w in (("dq", DSA_W), ("dv", DSA_W), ("qi", IDX_HEADS * IDX_DIM), ("wi", SUBLANE)):
    _TR[_name] = (_off, _off + _w)
    _off += _w
TR_ROWS = _off

INT_MIN = -2 ** 31
NEG_BIG = -1e30

f32 = jnp.float32
bf16 = jnp.bfloat16
i32 = jnp.int32


def _silu(x):
    return x * jax.nn.sigmoid(x)


def _dot(a, b):
    return jnp.dot(a, b, preferred_element_type=f32)


def _dot_nt(a, b):
    return lax.dot_general(a, b, (((1,), (1,)), ((), ())), preferred_element_type=f32)


def _dot_tn(a, b):
    return lax.dot_general(a, b, (((0,), (0,)), ((), ())), preferred_element_type=f32)


def _params():
    return pltpu.CompilerParams(dimension_semantics=("arbitrary", "arbitrary"),
                                vmem_limit_bytes=VMEM_LIMIT_BYTES)


def _in_proj_kernel(x_ref, g_ref, wn_ref, wt_ref,
                    gq_ref, gk_ref, gv_ref, gg_ref, gz_ref, dk_ref, kx_ref, dg_ref, pu_ref, pg_ref,
                    dqT_ref, dvT_ref, qiT_ref, wiT_ref):
    x = x_ref[...]
    h = x * lax.rsqrt(jnp.mean(x * x, axis=-1, keepdims=True) + EPS) * g_ref[...]
    hb = h.astype(bf16)
    for name, ref in (("gq", gq_ref), ("gk", gk_ref), ("gv", gv_ref), ("gg", gg_ref), ("gz", gz_ref),
                      ("dk", dk_ref), ("kx", kx_ref), ("dg", dg_ref), ("pu", pu_ref), ("pg", pg_ref)):
        lo, hi = _NAT[name]
        ref[...] = _dot(hb, wn_ref[:, lo:hi]).astype(ref.dtype)
    for name, ref in (("dq", dqT_ref), ("dv", dvT_ref), ("qi", qiT_ref), ("wi", wiT_ref)):
        lo, hi = _TR[name]
        y = _dot_nt(wt_ref[lo:hi, :], hb)
        if name == "dq":
            y = y * DSA_Q_SCALE
        ref[...] = y.astype(ref.dtype)


def _in_proj(x, g, wn, wt, tm):
    B, T, D = x.shape
    nat_dtypes = {"gq": f32, "gk": f32, "gv": f32, "gg": f32, "gz": f32,
                  "dk": bf16, "kx": bf16, "dg": f32, "pu": f32, "pg": f32}
    tr_dtypes = {"dq": bf16, "dv": bf16, "qi": bf16, "wi": f32}
    out_shape, out_specs = [], []
    for name, dt in nat_dtypes.items():
        w = _NAT[name][1] - _NAT[name][0]
        out_shape.append(jax.ShapeDtypeStruct((B, T, w), dt))
        out_specs.append(pl.BlockSpec((None, tm, w), lambda b, t: (b, t, 0)))
    for name, dt in tr_dtypes.items():
        w = _TR[name][1] - _TR[name][0]
        out_shape.append(jax.ShapeDtypeStruct((B, w, T), dt))
        out_specs.append(pl.BlockSpec((None, w, tm), lambda b, t: (b, 0, t)))
    return pl.pallas_call(
        _in_proj_kernel,
        grid=(B, T // tm),
        in_specs=[pl.BlockSpec((None, tm, D), lambda b, t: (b, t, 0)),
                  pl.BlockSpec((1, D), lambda b, t: (0, 0)),
                  pl.BlockSpec((D, NAT_COLS), lambda b, t: (0, 0)),
                  pl.BlockSpec((TR_ROWS, D), lambda b, t: (0, 0))],
        out_specs=out_specs,
        out_shape=out_shape,
        compiler_params=_params(),
        name="in_proj",
    )(x, g, wn, wt)


def _gla_kernel(gq_ref, gk_ref, gv_ref, gg_ref, gz_ref, w2_ref, b2_ref, gn_ref, out_ref,
                st_ref, glog_ref):
    C = GLA_CHUNK
    n_chunks = gq_ref.shape[0] // C

    @pl.when(pl.program_id(1) == 0)
    def _():
        st_ref[...] = jnp.zeros_like(st_ref)

    z = _dot(gz_ref[...].astype(bf16), w2_ref[...]) + b2_ref[...]
    glog_ref[...] = (jnp.minimum(z, 0.0) - jnp.log1p(jnp.exp(-jnp.abs(z)))) / GLA_GATE_NORM

    row = lax.broadcasted_iota(i32, (C, C), 0)
    col = lax.broadcasted_iota(i32, (C, C), 1)
    causal = col <= row
    tril = causal.astype(f32)
    gn = gn_ref[...]

    def chunk(c, carry):
        r = pl.ds(pl.multiple_of(c * C, C), C)
        for h in range(GLA_HEADS):
            cs = slice(h * LANE, (h + 1) * LANE)
            b = jnp.dot(tril, glog_ref[r, cs], precision=lax.Precision.HIGHEST,
                        preferred_element_type=f32)
            b_last = b[C - 1:C, :]
            q = gq_ref[r, cs]
            k = gk_ref[r, cs]
            v = gv_ref[r, cs].astype(bf16)
            qe = (q * jnp.exp(b) * (GLA_DK ** -0.5)).astype(bf16)
            ke = (k * jnp.exp(-b)).astype(bf16)
            kd = (k * jnp.exp(b_last - b)).astype(bf16)
            a = jnp.where(causal, _dot_nt(qe, ke), 0.0)
            st = st_ref[h]
            o = _dot(a.astype(bf16), v) + _dot_nt(qe, st.astype(bf16))
            st_ref[h] = st * jnp.exp(b_last) + _dot_tn(v, kd)
            ms = jnp.sum(o * o, axis=-1, keepdims=True) * (1.0 / GLA_DV)
            y = o * lax.rsqrt(ms + EPS) * gn * _silu(gg_ref[r, cs])
            out_ref[r, cs] = y.astype(out_ref.dtype)
        return carry

    lax.fori_loop(0, n_chunks, chunk, 0)


def _gla(gq, gk, gv, gg, gz, w2, b2, gn, tg):
    B, T, _ = gq.shape
    tok = lambda w: pl.BlockSpec((None, tg, w), lambda b, t: (b, t, 0))
    const = lambda s: pl.BlockSpec(s, lambda b, t: (0, 0))
    return pl.pallas_call(
        _gla_kernel,
        grid=(B, T // tg),
        in_specs=[tok(GLA_HP), tok(GLA_HP), tok(GLA_HP), tok(GLA_HP), tok(LANE),
                  const((LANE, GLA_HP)), const((1, GLA_HP)), const((1, LANE))],
        out_specs=tok(GLA_HP),
        out_shape=jax.ShapeDtypeStruct((B, T, GLA_HP), bf16),
        scratch_shapes=[pltpu.VMEM((GLA_HEADS, LANE, LANE), f32),
                        pltpu.VMEM((tg, GLA_HP), f32)],
        compiler_params=_params(),
        name="gla",
    )(gq, gk, gv, gg, gz, w2, b2, gn)


def _pool_kernel(pu_ref, pg_ref, w_ref, sc_ref, out_ref, halo_ref):
    tp = pu_ref.shape[0]
    H = POOL_HALO

    @pl.when(pl.program_id(1) == 0)
    def _():
        halo_ref[...] = jnp.zeros_like(halo_ref)

    u = pu_ref[...]
    ext = jnp.concatenate([halo_ref[...], u], axis=0)
    halo_ref[...] = u[tp - H:, :]
    s2 = ext + pltpu.roll(ext, 1, 0)
    s4 = s2 + pltpu.roll(s2, 2, 0)
    s8 = s4 + pltpu.roll(s4, 4, 0)
    s16 = s8 + pltpu.roll(s8, 8, 0)
    lane = lax.broadcasted_iota(i32, (tp, POOL_W), 1)
    grp = lane // POOL_GC
    sw = jnp.where(grp == 0, s2[H:], jnp.where(grp == 1, s4[H:], jnp.where(grp == 2, s8[H:], s16[H:])))
    win = jnp.where(grp == 0, POOL_WINDOWS[0],
                    jnp.where(grp == 1, POOL_WINDOWS[1],
                              jnp.where(grp == 2, POOL_WINDOWS[2], POOL_WINDOWS[3])))
    t = pl.program_id(1) * tp + lax.broadcasted_iota(i32, (tp, POOL_W), 0)
    cnt = jnp.minimum(t + 1, win).astype(f32)
    pooled = sw / cnt - u
    y = _dot(pooled.astype(bf16), w_ref[...]) * sc_ref[...]
    out_ref[...] = (y * _silu(pg_ref[...])).astype(out_ref.dtype)


def _pool(pu, pg, wbd, sc, tp):
    B, T, _ = pu.shape
    tok = pl.BlockSpec((None, tp, POOL_W), lambda b, t: (b, t, 0))
    return pl.pallas_call(
        _pool_kernel,
        grid=(B, T // tp),
        in_specs=[tok, tok,
                  pl.BlockSpec((POOL_W, POOL_W), lambda b, t: (0, 0)),
                  pl.BlockSpec((1, POOL_W), lambda b, t: (0, 0))],
        out_specs=tok,
        out_shape=jax.ShapeDtypeStruct((B, T, POOL_W), bf16),
        scratch_shapes=[pltpu.VMEM((POOL_HALO, POOL_W), f32)],
        compiler_params=_params(),
        name="pool",
    )(pu, pg, wbd, sc)


def _dsa_kernel(qT_ref, qiT_ref, wiT_ref, k_ref, kx_ref, vT_ref, dg_ref, bias_ref, out_ref,
                keys_ref, qm_ref, qim_ref, acc_ref, *, topk):
    tq = qT_ref.shape[1]
    tk = tq
    qb = pl.program_id(1)
    n_blocks = qb + 1

    def rows(kb):
        return pl.ds(pl.multiple_of(kb * tk, tk), tk)

    prow = lax.broadcasted_iota(i32, (LANE, tq), 0)
    for h in range(DSA_HEADS):
        pair = qT_ref[(h // 2) * LANE:(h // 2 + 1) * LANE, :]
        qm_ref[h] = jnp.where((prow // DSA_DH) == (h % 2), pair, jnp.zeros_like(pair))
    qi = qiT_ref[...]
    for h in range(IDX_HEADS):
        qim_ref[h] = jnp.where((prow // IDX_DIM) == h, qi, jnp.zeros_like(qi))
    w = wiT_ref[...] * ((IDX_DIM ** -0.5) * (IDX_HEADS ** -0.5))

    def score_keys(kb):
        kx = kx_ref[rows(kb), :]
        s = jnp.zeros((tk, tq), f32)
        for h in range(IDX_HEADS):
            s = s + jnp.maximum(_dot(kx, qim_ref[h]), 0.0) * w[h:h + 1, :]
        bits = lax.bitcast_convert_type(s, i32)
        return jnp.where(bits < 0, INT_MIN - bits, bits)

    def score_body(kb, c):
        keys_ref[rows(kb), :] = score_keys(kb)
        return c

    lax.fori_loop(0, qb, score_body, 0)
    krow = lax.broadcasted_iota(i32, (tk, tq), 0)
    qcol = lax.broadcasted_iota(i32, (tk, tq), 1)
    keys_ref[rows(qb), :] = jnp.where(krow <= qcol, score_keys(qb), INT_MIN)

    def count(pred):
        def body(kb, acc):
            c = pred(keys_ref[rows(kb), :]).astype(i32)
            return acc + jnp.sum(c.reshape(tk // SUBLANE, SUBLANE, tq), axis=0)
        acc = lax.fori_loop(0, n_blocks, body, jnp.zeros((SUBLANE, tq), i32))
        return jnp.sum(acc, axis=0, keepdims=True)

    def radix_step(i, prefix):
        cand = prefix + lax.shift_left(jnp.int32(1), 31 - i)
        return jnp.where(count(lambda blk: blk >= cand) >= topk, cand, prefix)

    tau = lax.fori_loop(0, 32, radix_step, jnp.full((1, tq), INT_MIN, i32))
    n_gt = count(lambda blk: blk > tau)
    need = jnp.where(tau == INT_MIN, 0, topk - n_gt).astype(f32)

    tri = (lax.broadcasted_iota(i32, (tk, tk), 1) <= lax.broadcasted_iota(i32, (tk, tk), 0)).astype(bf16)

    def mask_body(kb, run):
        keys = keys_ref[rows(kb), :]
        eq = keys == tau
        pre = _dot(tri, eq.astype(bf16)) + run
        sel = (keys > tau) | (eq & (pre <= need))
        keys_ref[rows(kb), :] = lax.bitcast_convert_type(jnp.where(sel, 0.0, NEG_BIG).astype(f32), i32)
        return pre[tk - 1:tk, :]

    lax.fori_loop(0, n_blocks, mask_body, jnp.zeros((1, tq), f32))

    acc_ref[...] = jnp.zeros_like(acc_ref)
    ones = jnp.ones((ONES_ROWS, tk), bf16)

    def qk(kb):
        return tuple(_dot(k_ref[rows(kb), (h // 2) * LANE:(h // 2 + 1) * LANE], qm_ref[h])
                     for h in range(DSA_HEADS))

    def attend(kb, carry, near):
        ms, dots = carry
        nxt = qk(jnp.minimum(kb + 1, qb))
        madd = lax.bitcast_convert_type(keys_ref[rows(kb), :], f32)
        new_ms = []
        for h in range(DSA_HEADS):
            lg = dots[h] + madd
            if near:
                j = kb - (qb - 1)
                lg = lg + bias_ref[h, pl.ds(pl.multiple_of(j * tk, tk), tk), :]
            m_new = jnp.maximum(ms[h], jnp.max(lg, axis=0, keepdims=True))
            alpha = jnp.exp2(ms[h] - m_new)
            p = jnp.exp2(lg - m_new).astype(bf16)
            v1 = jnp.concatenate([vT_ref[h * DSA_DH:(h + 1) * DSA_DH, rows(kb)], ones], axis=0)
            acc_ref[h] = alpha * acc_ref[h] + _dot(v1, p)
            new_ms.append(m_new)
        return tuple(new_ms), nxt

    n_far = jnp.maximum(qb - 1, 0)
    init = (tuple(jnp.full((1, tq), NEG_BIG, f32) for _ in range(DSA_HEADS)), qk(0))
    carry = lax.fori_loop(0, n_far, functools.partial(attend, near=False), init)
    lax.fori_loop(n_far, n_blocks, functools.partial(attend, near=True), carry)
    oT = jnp.concatenate([acc_ref[h, :DSA_DH, :] / acc_ref[h, DSA_DH:DSA_DH + 1, :]
                          for h in range(DSA_HEADS)], axis=0)
    out_ref[...] = (oT.T * _silu(dg_ref[...])).astype(out_ref.dtype)


def _dsa(dqT, qiT, wiT, dk, kx, dvT, dg, bias, tq, topk):
    B, T, _ = dk.shape
    whole = lambda shape, imap: pl.BlockSpec(shape, imap, pipeline_mode=pl.Buffered(1))
    return pl.pallas_call(
        functools.partial(_dsa_kernel, topk=topk),
        grid=(B, T // tq),
        in_specs=[pl.BlockSpec((None, DSA_W, tq), lambda b, q: (b, 0, q)),
                  pl.BlockSpec((None, IDX_HEADS * IDX_DIM, tq), lambda b, q: (b, 0, q)),
                  pl.BlockSpec((None, SUBLANE, tq), lambda b, q: (b, 0, q)),
                  whole((None, T, DSA_W), lambda b, q: (b, 0, 0)),
                  whole((None, T, LANE), lambda b, q: (b, 0, 0)),
                  whole((None, DSA_W, T), lambda b, q: (b, 0, 0)),
                  pl.BlockSpec((None, tq, DSA_W), lambda b, q: (b, q, 0)),
                  whole((DSA_HEADS, 2 * tq, tq), lambda b, q: (0, 0, 0))],
        out_specs=pl.BlockSpec((None, tq, DSA_W), lambda b, q: (b, q, 0)),
        out_shape=jax.ShapeDtypeStruct((B, T, DSA_W), bf16),
        scratch_shapes=[pltpu.VMEM((T, tq), i32),
                        pltpu.VMEM((DSA_HEADS, LANE, tq), bf16),
                        pltpu.VMEM((IDX_HEADS, LANE, tq), bf16),
                        pltpu.VMEM((DSA_HEADS, DSA_DH + ONES_ROWS, tq), f32)],
        compiler_params=_params(),
        name="dsa",
    )(dqT, qiT, wiT, dk, kx, dvT, dg, bias)


def _out_proj_kernel(x_ref, yg_ref, yd_ref, yp_ref, wg_ref, wd_ref, wp_ref, fg_ref, out_ref, *, final):
    x = x_ref[...] + _dot(yg_ref[...], wg_ref[...]) + _dot(yd_ref[...], wd_ref[...]) \
        + _dot(yp_ref[...], wp_ref[...])
    if final:
        x = x * lax.rsqrt(jnp.mean(x * x, axis=-1, keepdims=True) + EPS) * fg_ref[...]
    out_ref[...] = x


def _out_proj(x, yg, yd, yp, wg, wd, wp, fg, tm, final):
    B, T, D = x.shape
    tok = lambda w: pl.BlockSpec((None, tm, w), lambda b, t: (b, t, 0))
    const = lambda s: pl.BlockSpec(s, lambda b, t: (0, 0))
    return pl.pallas_call(
        functools.partial(_out_proj_kernel, final=final),
        grid=(B, T // tm),
        in_specs=[tok(D), tok(GLA_HP), tok(DSA_W), tok(POOL_W),
                  const((GLA_HP, D)), const((DSA_W, D)), const((POOL_W, D)), const((1, D))],
        out_specs=tok(D),
        out_shape=jax.ShapeDtypeStruct((B, T, D), f32),
        compiler_params=_params(),
        name="out_proj_final" if final else "out_proj",
    )(x, yg, yd, yp, wg, wd, wp, fg)


def _pad_heads(w, heads, d):
    lead = w.shape[:-1]
    w = w.reshape(*lead, heads, d)
    w = jnp.pad(w, [(0, 0)] * len(lead) + [(0, 0), (0, LANE - d)])
    return w.reshape(*lead, heads * LANE)


def _split_cols(w):
    outs, off = [], 0
    for s in IN_SIZES:
        outs.append(w[..., off:off + s])
        off += s
    return outs


def _layer_weights(w_in, gate_w2, gate_b, gla_norm_g, pool_w, pool_scale, w_out):
    (gq, gk, gv, gz, gg, dq, dk, dv, dg, qi, ki, wi, pu, pg) = _split_cols(w_in)
    D = w_in.shape[0]
    wn = jnp.concatenate([
        _pad_heads(gq, GLA_HEADS, GLA_DK), _pad_heads(gk, GLA_HEADS, GLA_DK),
        _pad_heads(gv, GLA_HEADS, GLA_DV), _pad_heads(gg, GLA_HEADS, GLA_DV),
        jnp.pad(gz, ((0, 0), (0, LANE - GLA_GATE_RANK))),
        dk, jnp.tile(ki, (1, IDX_HEADS)), dg, pu, pg], axis=1).astype(bf16)
    wt = jnp.concatenate([
        dq, dv, qi, jnp.pad(wi, ((0, 0), (0, SUBLANE - IDX_HEADS)))], axis=1).T.astype(bf16)
    w2 = jnp.pad(_pad_heads(gate_w2, GLA_HEADS, GLA_DK), ((0, LANE - GLA_GATE_RANK), (0, 0))).astype(bf16)
    b2 = _pad_heads(gate_b[None, :], GLA_HEADS, GLA_DK)
    gn = jnp.pad(gla_norm_g[None, :], ((0, 0), (0, LANE - GLA_DV)))
    wbd = jax.scipy.linalg.block_diag(*[pool_w[g] for g in range(POOL_GROUPS)]).astype(bf16)
    sc = pool_scale[None, :]
    wg = _pad_heads(w_out[:GLA_W].T, GLA_HEADS, GLA_DV).T.astype(bf16)
    wd = w_out[GLA_W:GLA_W + DSA_W].astype(bf16)
    wp = w_out[GLA_W + DSA_W:].astype(bf16)
    return wn, wt, w2, b2, gn, wbd, sc, wg, wd, wp


def _t5_bucket_table(n):
    rel = np.arange(n)
    max_exact = REL_BUCKETS // 2
    relf = np.maximum(rel, 1).astype(np.float32)
    large = max_exact + (np.log(relf / np.float32(max_exact)) / np.float32(math.log(REL_MAX_DIST / max_exact))
                         * np.float32(REL_BUCKETS - max_exact)).astype(np.int32)
    large = np.minimum(large, REL_BUCKETS - 1)
    return np.where(rel < max_exact, rel, large)


def _near_bias(rel_bias, tq):
    bucket = _t5_bucket_table(2 * tq)
    assert np.all(bucket[tq + 1:] == REL_BUCKETS - 1)
    j = np.arange(2 * tq)[:, None]
    i = np.arange(tq)[None, :]
    rel = np.maximum(tq + i - j, 0)
    by_rel = (rel_bias[bucket] - rel_bias[REL_BUCKETS - 1][None, :]) * LOG2E
    return jnp.transpose(by_rel[rel], (2, 0, 1))


def kernel(x, norm_g, w_in, gla_gate_w2, gla_gate_b, gla_norm_g, rel_bias, pool_w, pool_scale, w_out,
           final_norm_g):
    B, T, D = x.shape
    assert D == D_MODEL
    tm = min(512, T)
    tq = min(256, T)
    topk = min(TOPK_MAX, T // 4)
    assert T % tm == 0 and T % tq == 0 and tq >= LANE and T % GLA_CHUNK == 0
    bias = _near_bias(rel_bias.astype(f32), tq)
    fg = final_norm_g[None, :]
    for l in range(DEPTH):
        wn, wt, w2, b2, gn, wbd, sc, wg, wd, wp = _layer_weights(
            w_in[l], gla_gate_w2[l], gla_gate_b[l], gla_norm_g[l], pool_w[l], pool_scale[l], w_out[l])
        (gq, gk, gv, gg, gz, dk, kx, dg, pu, pg, dqT, dvT, qiT, wiT) = _in_proj(x, norm_g[l][None, :], wn, wt, tm)
        yg = _gla(gq, gk, gv, gg, gz, w2, b2, gn, tm)
        yp = _pool(pu, pg, wbd, sc, tm)
        yd = _dsa(dqT, qiT, wiT, dk, kx, dvT, dg, bias, tq, topk)
        x = _out_proj(x, yg, yd, yp, wg, wd, wp, fg, tm, final=(l == DEPTH - 1))
    return x
```

```python
import functools
import math

import numpy as np
import jax
import jax.numpy as jnp
from jax import lax
from jax.experimental import pallas as pl
from jax.experimental.pallas import tpu as pltpu

D_MODEL = 1024
DEPTH = 4
EPS = 1e-6
GLA_HEADS = 4
GLA_DK = 48
GLA_DV = 96
GLA_GATE_RANK = 16
GLA_GATE_NORM = 16.0
GLA_CHUNK = 64
DSA_HEADS = 6
DSA_DH = 64
IDX_HEADS = 4
IDX_DIM = 32
TOPK_MAX = 256
POOL_GROUPS = 4
POOL_GC = 64
POOL_WINDOWS = (2, 4, 8, 16)
REL_BUCKETS = 32
REL_MAX_DIST = 128

GLA_W = GLA_HEADS * GLA_DV
DSA_W = DSA_HEADS * DSA_DH
POOL_W = POOL_GROUPS * POOL_GC
IN_SIZES = (GLA_HEADS * GLA_DK, GLA_HEADS * GLA_DK, GLA_W, GLA_GATE_RANK, GLA_W,
            DSA_W, DSA_W, DSA_W, DSA_W, IDX_HEADS * IDX_DIM, IDX_DIM, IDX_HEADS,
            POOL_W, POOL_W)

LANE = 128
SUBLANE = 8
VMEM_LIMIT_BYTES = 56 * 1024 * 1024

GLA_HP = GLA_HEADS * LANE
POOL_HALO = 16
ONES_ROWS = 16
LOG2E = math.log2(math.e)
DSA_Q_SCALE = (DSA_DH ** -0.5) * LOG2E

_NAT = {}
_off = 0
for _name, _w in (("gq", GLA_HP), ("gk", GLA_HP), ("gv", GLA_HP), ("gg", GLA_HP), ("gz", LANE),
                  ("dk", DSA_W), ("kx", LANE), ("dg", DSA_W), ("pu", POOL_W), ("pg", POOL_W)):
    _NAT[_name] = (_off, _off + _w)
    _off += _w
NAT_COLS = _off
_TR = {}
_off = 0
for _name, _w in (("dq", DSA_W), ("dv", DSA_W), ("qi", IDX_HEADS * IDX_DIM), ("wi", SUBLANE)):
    _TR[_name] = (_off, _off + _w)
    _off += _w
TR_ROWS = _off

INT_MIN = -2 ** 31
I16_MIN = -2 ** 15
I16_ROWS = 16
NEG_BIG = -1e30

f32 = jnp.float32
bf16 = jnp.bfloat16
i32 = jnp.int32
i16 = jnp.int16


def _silu(x):
    return x * jax.nn.sigmoid(x)


def _dot(a, b):
    return jnp.dot(a, b, preferred_element_type=f32)


def _dot_nt(a, b):
    return lax.dot_general(a, b, (((1,), (1,)), ((), ())), preferred_element_type=f32)


def _dot_tn(a, b):
    return lax.dot_general(a, b, (((0,), (0,)), ((), ())), preferred_element_type=f32)


def _params():
    return pltpu.CompilerParams(dimension_semantics=("arbitrary", "arbitrary"),
                                vmem_limit_bytes=VMEM_LIMIT_BYTES)


def _in_proj_kernel(x_ref, g_ref, wn_ref, wt_ref,
                    gq_ref, gk_ref, gv_ref, gg_ref, gz_ref, dk_ref, kx_ref, dg_ref, pu_ref, pg_ref,
                    dqT_ref, dvT_ref, qiT_ref, wiT_ref):
    x = x_ref[...]
    h = x * lax.rsqrt(jnp.mean(x * x, axis=-1, keepdims=True) + EPS) * g_ref[...]
    hb = h.astype(bf16)
    for name, ref in (("gq", gq_ref), ("gk", gk_ref), ("gv", gv_ref), ("gg", gg_ref), ("gz", gz_ref),
                      ("dk", dk_ref), ("kx", kx_ref), ("dg", dg_ref), ("pu", pu_ref), ("pg", pg_ref)):
        lo, hi = _NAT[name]
        ref[...] = _dot(hb, wn_ref[:, lo:hi]).astype(ref.dtype)
    for name, ref in (("dq", dqT_ref), ("dv", dvT_ref), ("qi", qiT_ref), ("wi", wiT_ref)):
        lo, hi = _TR[name]
        y = _dot_nt(wt_ref[lo:hi, :], hb)
        if name == "dq":
            y = y * DSA_Q_SCALE
        ref[...] = y.astype(ref.dtype)


def _in_proj(x, g, wn, wt, tm):
    B, T, D = x.shape
    nat_dtypes = {"gq": f32, "gk": f32, "gv": f32, "gg": f32, "gz": f32,
                  "dk": bf16, "kx": bf16, "dg": f32, "pu": f32, "pg": f32}
    tr_dtypes = {"dq": bf16, "dv": bf16, "qi": bf16, "wi": f32}
    out_shape, out_specs = [], []
    for name, dt in nat_dtypes.items():
        w = _NAT[name][1] - _NAT[name][0]
        out_shape.append(jax.ShapeDtypeStruct((B, T, w), dt))
        out_specs.append(pl.BlockSpec((None, tm, w), lambda b, t: (b, t, 0)))
    for name, dt in tr_dtypes.items():
        w = _TR[name][1] - _TR[name][0]
        out_shape.append(jax.ShapeDtypeStruct((B, w, T), dt))
        out_specs.append(pl.BlockSpec((None, w, tm), lambda b, t: (b, 0, t)))
    return pl.pallas_call(
        _in_proj_kernel,
        grid=(B, T // tm),
        in_specs=[pl.BlockSpec((None, tm, D), lambda b, t: (b, t, 0)),
                  pl.BlockSpec((1, D), lambda b, t: (0, 0)),
                  pl.BlockSpec((D, NAT_COLS), lambda b, t: (0, 0)),
                  pl.BlockSpec((TR_ROWS, D), lambda b, t: (0, 0))],
        out_specs=out_specs,
        out_shape=out_shape,
        compiler_params=_params(),
        name="in_proj",
    )(x, g, wn, wt)


def _gla_kernel(gq_ref, gk_ref, gv_ref, gg_ref, gz_ref, w2_ref, b2_ref, gn_ref, out_ref,
                st_ref, glog_ref):
    C = GLA_CHUNK
    n_chunks = gq_ref.shape[0] // C

    @pl.when(pl.program_id(1) == 0)
    def _():
        st_ref[...] = jnp.zeros_like(st_ref)

    z = _dot(gz_ref[...].astype(bf16), w2_ref[...]) + b2_ref[...]
    b = (jnp.minimum(z, 0.0) - jnp.log1p(jnp.exp(-jnp.abs(z)))) / GLA_GATE_NORM
    pos = lax.broadcasted_iota(i32, b.shape, 0) & (C - 1)
    shift = 1
    while shift < C:
        b = b + jnp.where(pos >= shift, pltpu.roll(b, shift, 0), 0.0)
        shift *= 2
    glog_ref[...] = b

    causal = lax.broadcasted_iota(i32, (C, C), 1) <= lax.broadcasted_iota(i32, (C, C), 0)
    gn = gn_ref[...]
    heads = range(GLA_HEADS)
    cols = [slice(h * LANE, (h + 1) * LANE) for h in heads]

    def chunk(c, carry):
        r = pl.ds(pl.multiple_of(c * C, C), C)
        bs = [glog_ref[r, cs] for cs in cols]
        b_last = [bb[C - 1:C, :] for bb in bs]
        ks = [gk_ref[r, cs] for cs in cols]
        vs = [gv_ref[r, cs].astype(bf16) for cs in cols]
        qe = [(gq_ref[r, cols[h]] * jnp.exp(bs[h]) * (GLA_DK ** -0.5)).astype(bf16) for h in heads]
        ke = [(ks[h] * jnp.exp(-bs[h])).astype(bf16) for h in heads]
        kd = [(ks[h] * jnp.exp(b_last[h] - bs[h])).astype(bf16) for h in heads]
        st = [st_ref[h] for h in heads]
        a = [_dot_nt(qe[h], ke[h]) for h in heads]
        inter = [_dot_nt(qe[h], st[h].astype(bf16)) for h in heads]
        u = [_dot_tn(vs[h], kd[h]) for h in heads]
        o = [_dot(jnp.where(causal, a[h], 0.0).astype(bf16), vs[h]) + inter[h] for h in heads]
        for h in heads:
            st_ref[h] = st[h] * jnp.exp(b_last[h]) + u[h]
            ms = jnp.sum(o[h] * o[h], axis=-1, keepdims=True) * (1.0 / GLA_DV)
            y = o[h] * lax.rsqrt(ms + EPS) * gn * _silu(gg_ref[r, cols[h]])
            out_ref[r, cols[h]] = y.astype(out_ref.dtype)
        return carry

    lax.fori_loop(0, n_chunks, chunk, 0)


def _gla(gq, gk, gv, gg, gz, w2, b2, gn, tg):
    B, T, _ = gq.shape
    tok = lambda w: pl.BlockSpec((None, tg, w), lambda b, t: (b, t, 0))
    const = lambda s: pl.BlockSpec(s, lambda b, t: (0, 0))
    return pl.pallas_call(
        _gla_kernel,
        grid=(B, T // tg),
        in_specs=[tok(GLA_HP), tok(GLA_HP), tok(GLA_HP), tok(GLA_HP), tok(LANE),
                  const((LANE, GLA_HP)), const((1, GLA_HP)), const((1, LANE))],
        out_specs=tok(GLA_HP),
        out_shape=jax.ShapeDtypeStruct((B, T, GLA_HP), bf16),
        scratch_shapes=[pltpu.VMEM((GLA_HEADS, LANE, LANE), f32),
                        pltpu.VMEM((tg, GLA_HP), f32)],
        compiler_params=_params(),
        name="gla",
    )(gq, gk, gv, gg, gz, w2, b2, gn)


def _pool_kernel(pu_ref, pg_ref, w_ref, sc_ref, out_ref, halo_ref):
    tp = pu_ref.shape[0]
    H = POOL_HALO

    @pl.when(pl.program_id(1) == 0)
    def _():
        halo_ref[...] = jnp.zeros_like(halo_ref)

    u = pu_ref[...]
    ext = jnp.concatenate([halo_ref[...], u], axis=0)
    halo_ref[...] = u[tp - H:, :]
    s2 = ext + pltpu.roll(ext, 1, 0)
    s4 = s2 + pltpu.roll(s2, 2, 0)
    s8 = s4 + pltpu.roll(s4, 4, 0)
    s16 = s8 + pltpu.roll(s8, 8, 0)
    lane = lax.broadcasted_iota(i32, (tp, POOL_W), 1)
    grp = lane // POOL_GC
    sw = jnp.where(grp == 0, s2[H:], jnp.where(grp == 1, s4[H:], jnp.where(grp == 2, s8[H:], s16[H:])))
    win = jnp.where(grp == 0, POOL_WINDOWS[0],
                    jnp.where(grp == 1, POOL_WINDOWS[1],
                              jnp.where(grp == 2, POOL_WINDOWS[2], POOL_WINDOWS[3])))
    t = pl.program_id(1) * tp + lax.broadcasted_iota(i32, (tp, POOL_W), 0)
    cnt = jnp.minimum(t + 1, win).astype(f32)
    pooled = sw / cnt - u
    y = _dot(pooled.astype(bf16), w_ref[...]) * sc_ref[...]
    out_ref[...] = (y * _silu(pg_ref[...])).astype(out_ref.dtype)


def _pool(pu, pg, wbd, sc, tp):
    B, T, _ = pu.shape
    tok = pl.BlockSpec((None, tp, POOL_W), lambda b, t: (b, t, 0))
    return pl.pallas_call(
        _pool_kernel,
        grid=(B, T // tp),
        in_specs=[tok, tok,
                  pl.BlockSpec((POOL_W, POOL_W), lambda b, t: (0, 0)),
                  pl.BlockSpec((1, POOL_W), lambda b, t: (0, 0))],
        out_specs=tok,
        out_shape=jax.ShapeDtypeStruct((B, T, POOL_W), bf16),
        scratch_shapes=[pltpu.VMEM((POOL_HALO, POOL_W), f32)],
        compiler_params=_params(),
        name="pool",
    )(pu, pg, wbd, sc)


def _dsa_kernel(qT_ref, qiT_ref, wiT_ref, k_ref, kx_ref, vT_ref, dg_ref, bias_ref, out_ref,
                keys_ref, hi_ref, lo_ref, qm_ref, qim_ref, acc_ref, *, topk):
    tq = qT_ref.shape[1]
    tk = tq
    qb = pl.program_id(1)
    n_blocks = qb + 1

    def rows(kb):
        return pl.ds(pl.multiple_of(kb * tk, tk), tk)

    prow = lax.broadcasted_iota(i32, (LANE, tq), 0)
    for h in range(DSA_HEADS):
        pair = qT_ref[(h // 2) * LANE:(h // 2 + 1) * LANE, :]
        qm_ref[h // 2, :, (h % 2) * tq:(h % 2 + 1) * tq] = jnp.where(
            (prow // DSA_DH) == (h % 2), pair, jnp.zeros_like(pair))
    qi = qiT_ref[...]
    for h in range(IDX_HEADS):
        qim_ref[:, h * tq:(h + 1) * tq] = jnp.where((prow // IDX_DIM) == h, qi, jnp.zeros_like(qi))
    w = wiT_ref[...] * ((IDX_DIM ** -0.5) * (IDX_HEADS ** -0.5))

    def score_keys(r):
        d = _dot(kx_ref[r, :], qim_ref[...])
        s = jnp.zeros((d.shape[0], tq), f32)
        for h in range(IDX_HEADS):
            s = s + jnp.maximum(d[:, h * tq:(h + 1) * tq], 0.0) * w[h:h + 1, :]
        bits = lax.bitcast_convert_type(s, i32)
        return jnp.where(bits < 0, INT_MIN - bits, bits)

    def store_keys(r, keys):
        keys_ref[r, :] = keys
        hi_ref[r, :] = (keys >> 16).astype(i16)
        lo_ref[r, :] = (keys ^ 0x8000).astype(i16)

    def tile(i, span):
        return pl.ds(pl.multiple_of(i * span, span), span)

    def score_body(i, c, span):
        store_keys(tile(i, span), score_keys(tile(i, span)))
        return c

    lax.fori_loop(0, qb // 2, functools.partial(score_body, span=2 * tk), 0)
    lax.fori_loop(2 * (qb // 2), qb, functools.partial(score_body, span=tk), 0)
    krow = lax.broadcasted_iota(i32, (tk, tq), 0)
    qcol = lax.broadcasted_iota(i32, (tk, tq), 1)
    store_keys(rows(qb), jnp.where(krow <= qcol, score_keys(rows(qb)), INT_MIN))

    @pl.when(n_blocks % 2 == 1)
    def _():
        store_keys(rows(n_blocks), jnp.full((tk, tq), INT_MIN, i32))

    n_pairs = (n_blocks + 1) // 2

    def count16(ref, pred):
        def body(p, acc):
            blk = ref[pl.ds(pl.multiple_of(p * (2 * tk), 2 * tk), 2 * tk), :]
            c = jnp.where(pred(blk), jnp.int16(1), jnp.int16(0)).reshape(2 * tk // I16_ROWS, I16_ROWS, tq)
            parts = [c[i] for i in range(c.shape[0])]
            while len(parts) > 1:
                parts = [parts[i] + parts[i + 1] for i in range(0, len(parts), 2)]
            return acc + parts[0]
        acc = lax.fori_loop(0, n_pairs, body, jnp.zeros((I16_ROWS, tq), i16))
        return jnp.sum(acc.astype(i32), axis=0, keepdims=True)

    def radix16(ref, k):
        def step(i, prefix):
            cand = prefix + lax.shift_left(jnp.int32(1), 15 - i)
            c16 = cand.astype(i16)
            return jnp.where(count16(ref, lambda blk: blk >= c16) >= k, cand, prefix)
        return lax.fori_loop(0, 16, step, jnp.full((1, tq), I16_MIN, i32))

    t_hi = radix16(hi_ref, topk)
    t_hi16 = t_hi.astype(i16)
    n_gt_hi = count16(hi_ref, lambda blk: blk > t_hi16)

    def narrow_body(p, c):
        r = pl.ds(pl.multiple_of(p * (2 * tk), 2 * tk), 2 * tk)
        hi_ref[r, :] = jnp.where(hi_ref[r, :] == t_hi16, lo_ref[r, :], jnp.int16(I16_MIN))
        return c

    lax.fori_loop(0, n_pairs, narrow_body, 0)
    t_lo = radix16(hi_ref, topk - n_gt_hi)
    t_lo16 = t_lo.astype(i16)
    n_gt = n_gt_hi + count16(hi_ref, lambda blk: blk > t_lo16)
    tau = t_hi * 65536 + (t_lo - I16_MIN)
    need = jnp.where(tau == INT_MIN, 0, topk - n_gt).astype(f32)

    tri = (lax.broadcasted_iota(i32, (tk, tk), 1) <= lax.broadcasted_iota(i32, (tk, tk), 0)).astype(bf16)

    def mask_body(p, run):
        ks = [keys_ref[rows(2 * p + half), :] for half in range(2)]
        eqs = [k == tau for k in ks]
        cnt = [_dot(tri, e.astype(bf16)) for e in eqs]
        for half in range(2):
            pre = cnt[half] + run
            sel = (ks[half] > tau) | (eqs[half] & (pre <= need))
            keys_ref[rows(2 * p + half), :] = lax.bitcast_convert_type(
                jnp.where(sel, 0.0, NEG_BIG).astype(f32), i32)
            run = pre[tk - 1:tk, :]
        return run

    lax.fori_loop(0, n_pairs, mask_body, jnp.zeros((1, tq), f32))

    acc_ref[...] = jnp.zeros_like(acc_ref)

    def attend(i, ms, span, near):
        r = pl.ds(pl.multiple_of(i * span, span), span)
        ones = jnp.ones((ONES_ROWS, span), bf16)
        madd = lax.bitcast_convert_type(keys_ref[r, :], f32)
        dots = [_dot(k_ref[r, g * LANE:(g + 1) * LANE], qm_ref[g]) for g in range(DSA_HEADS // 2)]
        new_ms = []
        for h in range(DSA_HEADS):
            lg = dots[h // 2][:, (h % 2) * tq:(h % 2 + 1) * tq] + madd
            if near:
                j = i - (qb - 2)
                lg = lg + bias_ref[h, pl.ds(pl.multiple_of(j * tk, tk), tk), :]
            m_new = jnp.maximum(ms[h], jnp.max(lg, axis=0, keepdims=True))
            alpha = jnp.exp2(ms[h] - m_new)
            p = jnp.exp2(lg - m_new).astype(bf16)
            v1 = jnp.concatenate([vT_ref[h * DSA_DH:(h + 1) * DSA_DH, r], ones], axis=0)
            acc_ref[h] = alpha * acc_ref[h] + _dot(v1, p)
            new_ms.append(m_new)
        return tuple(new_ms)

    n_far_pairs = jnp.maximum(qb - 1, 0) // 2
    ms = tuple(jnp.full((1, tq), NEG_BIG, f32) for _ in range(DSA_HEADS))
    ms = lax.fori_loop(0, n_far_pairs, functools.partial(attend, span=2 * tk, near=False), ms)
    lax.fori_loop(2 * n_far_pairs, n_blocks, functools.partial(attend, span=tk, near=True), ms)
    oT = jnp.concatenate([acc_ref[h, :DSA_DH, :] / acc_ref[h, DSA_DH:DSA_DH + 1, :]
                          for h in range(DSA_HEADS)], axis=0)
    out_ref[...] = (oT.T * _silu(dg_ref[...])).astype(out_ref.dtype)


def _dsa(dqT, qiT, wiT, dk, kx, dvT, dg, bias, tq, topk):
    B, T, _ = dk.shape
    whole = lambda shape, imap: pl.BlockSpec(shape, imap, pipeline_mode=pl.Buffered(1))
    return pl.pallas_call(
        functools.partial(_dsa_kernel, topk=topk),
        grid=(B, T // tq),
        in_specs=[pl.BlockSpec((None, DSA_W, tq), lambda b, q: (b, 0, q)),
                  pl.BlockSpec((None, IDX_HEADS * IDX_DIM, tq), lambda b, q: (b, 0, q)),
                  pl.BlockSpec((None, SUBLANE, tq), lambda b, q: (b, 0, q)),
                  whole((None, T, DSA_W), lambda b, q: (b, 0, 0)),
                  whole((None, T, LANE), lambda b, q: (b, 0, 0)),
                  whole((None, DSA_W, T), lambda b, q: (b, 0, 0)),
                  pl.BlockSpec((None, tq, DSA_W), lambda b, q: (b, q, 0)),
                  whole((DSA_HEADS, 3 * tq, tq), lambda b, q: (0, 0, 0))],
        out_specs=pl.BlockSpec((None, tq, DSA_W), lambda b, q: (b, q, 0)),
        out_shape=jax.ShapeDtypeStruct((B, T, DSA_W), bf16),
        scratch_shapes=[pltpu.VMEM((T + tq, tq), i32),
                        pltpu.VMEM((T + tq, tq), i16),
                        pltpu.VMEM((T + tq, tq), i16),
                        pltpu.VMEM((DSA_HEADS // 2, LANE, 2 * tq), bf16),
                        pltpu.VMEM((LANE, IDX_HEADS * tq), bf16),
                        pltpu.VMEM((DSA_HEADS, DSA_DH + ONES_ROWS, tq), f32)],
        compiler_params=_params(),
        name="dsa",
    )(dqT, qiT, wiT, dk, kx, dvT, dg, bias)


def _out_proj_kernel(x_ref, yg_ref, yd_ref, yp_ref, wg_ref, wd_ref, wp_ref, fg_ref, out_ref, *, final):
    x = x_ref[...] + _dot(yg_ref[...], wg_ref[...]) + _dot(yd_ref[...], wd_ref[...]) \
        + _dot(yp_ref[...], wp_ref[...])
    if final:
        x = x * lax.rsqrt(jnp.mean(x * x, axis=-1, keepdims=True) + EPS) * fg_ref[...]
    out_ref[...] = x


def _out_proj(x, yg, yd, yp, wg, wd, wp, fg, tm, final):
    B, T, D = x.shape
    tok = lambda w: pl.BlockSpec((None, tm, w), lambda b, t: (b, t, 0))
    const = lambda s: pl.BlockSpec(s, lambda b, t: (0, 0))
    return pl.pallas_call(
        functools.partial(_out_proj_kernel, final=final),
        grid=(B, T // tm),
        in_specs=[tok(D), tok(GLA_HP), tok(DSA_W), tok(POOL_W),
                  const((GLA_HP, D)), const((DSA_W, D)), const((POOL_W, D)), const((1, D))],
        out_specs=tok(D),
        out_shape=jax.ShapeDtypeStruct((B, T, D), f32),
        compiler_params=_params(),
        name="out_proj_final" if final else "out_proj",
    )(x, yg, yd, yp, wg, wd, wp, fg)


def _pad_heads(w, heads, d):
    lead = w.shape[:-1]
    w = w.reshape(*lead, heads, d)
    w = jnp.pad(w, [(0, 0)] * len(lead) + [(0, 0), (0, LANE - d)])
    return w.reshape(*lead, heads * LANE)


def _split_cols(w):
    outs, off = [], 0
    for s in IN_SIZES:
        outs.append(w[..., off:off + s])
        off += s
    return outs


def _layer_weights(w_in, gate_w2, gate_b, gla_norm_g, pool_w, pool_scale, w_out):
    (gq, gk, gv, gz, gg, dq, dk, dv, dg, qi, ki, wi, pu, pg) = _split_cols(w_in)
    D = w_in.shape[0]
    wn = jnp.concatenate([
        _pad_heads(gq, GLA_HEADS, GLA_DK), _pad_heads(gk, GLA_HEADS, GLA_DK),
        _pad_heads(gv, GLA_HEADS, GLA_DV), _pad_heads(gg, GLA_HEADS, GLA_DV),
        jnp.pad(gz, ((0, 0), (0, LANE - GLA_GATE_RANK))),
        dk, jnp.tile(ki, (1, IDX_HEADS)), dg, pu, pg], axis=1).astype(bf16)
    wt = jnp.concatenate([
        dq, dv, qi, jnp.pad(wi, ((0, 0), (0, SUBLANE - IDX_HEADS)))], axis=1).T.astype(bf16)
    w2 = jnp.pad(_pad_heads(gate_w2, GLA_HEADS, GLA_DK), ((0, LANE - GLA_GATE_RANK), (0, 0))).astype(bf16)
    b2 = _pad_heads(gate_b[None, :], GLA_HEADS, GLA_DK)
    gn = jnp.pad(gla_norm_g[None, :], ((0, 0), (0, LANE - GLA_DV)))
    wbd = jax.scipy.linalg.block_diag(*[pool_w[g] for g in range(POOL_GROUPS)]).astype(bf16)
    sc = pool_scale[None, :]
    wg = _pad_heads(w_out[:GLA_W].T, GLA_HEADS, GLA_DV).T.astype(bf16)
    wd = w_out[GLA_W:GLA_W + DSA_W].astype(bf16)
    wp = w_out[GLA_W + DSA_W:].astype(bf16)
    return wn, wt, w2, b2, gn, wbd, sc, wg, wd, wp


def _t5_bucket_table(n):
    rel = np.arange(n)
    max_exact = REL_BUCKETS // 2
    relf = np.maximum(rel, 1).astype(np.float32)
    large = max_exact + (np.log(relf / np.float32(max_exact)) / np.float32(math.log(REL_MAX_DIST / max_exact))
                         * np.float32(REL_BUCKETS - max_exact)).astype(np.int32)
    large = np.minimum(large, REL_BUCKETS - 1)
    return np.where(rel < max_exact, rel, large)


def _near_bias(rel_bias, tq):
    bucket = _t5_bucket_table(3 * tq)
    assert np.all(bucket[tq + 1:] == REL_BUCKETS - 1)
    by_rel = (rel_bias[bucket] - rel_bias[REL_BUCKETS - 1][None, :]) * LOG2E
    n = 4 * tq - 1
    g = by_rel[np.maximum(np.arange(n) - (tq - 1), 0)].T
    reps = -(-(3 * tq * (n + 1)) // n)
    skew = jnp.tile(g, (1, reps))[:, :3 * tq * (n + 1)].reshape(-1, 3 * tq, n + 1)
    return skew[:, ::-1, :tq]


def kernel(x, norm_g, w_in, gla_gate_w2, gla_gate_b, gla_norm_g, rel_bias, pool_w, pool_scale, w_out,
           final_norm_g):
    B, T, D = x.shape
    assert D == D_MODEL
    tm = min(512, T)
    tq = min(256, T)
    topk = min(TOPK_MAX, T // 4)
    assert T % tm == 0 and T % (2 * tq) == 0 and tq >= LANE and T % GLA_CHUNK == 0
    bias = _near_bias(rel_bias.astype(f32), tq)
    fg = final_norm_g[None, :]
    for l in range(DEPTH):
        wn, wt, w2, b2, gn, wbd, sc, wg, wd, wp = _layer_weights(
            w_in[l], gla_gate_w2[l], gla_gate_b[l], gla_norm_g[l], pool_w[l], pool_scale[l], w_out[l])
        (gq, gk, gv, gg, gz, dk, kx, dg, pu, pg, dqT, dvT, qiT, wiT) = _in_proj(x, norm_g[l][None, :], wn, wt, tm)
        yg = _gla(gq, gk, gv, gg, gz, w2, b2, gn, tm)
        yp = _pool(pu, pg, wbd, sc, tm)
        yd = _dsa(dqT, qiT, wiT, dk, kx, dvT, dg, bias, tq, topk)
        x = _out_proj(x, yg, yd, yp, wg, wd, wp, fg, tm, final=(l == DEPTH - 1))
    return x
```

```python
import functools
import math

import numpy as np
import jax
import jax.numpy as jnp
from jax import lax
from jax.experimental import pallas as pl
from jax.experimental.pallas import tpu as pltpu

D_MODEL = 1024
DEPTH = 4
EPS = 1e-6
GLA_HEADS = 4
GLA_DK = 48
GLA_DV = 96
GLA_GATE_RANK = 16
GLA_GATE_NORM = 16.0
GLA_CHUNK = 64
DSA_HEADS = 6
DSA_DH = 64
IDX_HEADS = 4
IDX_DIM = 32
TOPK_MAX = 256
POOL_GROUPS = 4
POOL_GC = 64
POOL_WINDOWS = (2, 4, 8, 16)
REL_BUCKETS = 32
REL_MAX_DIST = 128

GLA_W = GLA_HEADS * GLA_DV
DSA_W = DSA_HEADS * DSA_DH
POOL_W = POOL_GROUPS * POOL_GC
IN_SIZES = (GLA_HEADS * GLA_DK, GLA_HEADS * GLA_DK, GLA_W, GLA_GATE_RANK, GLA_W,
            DSA_W, DSA_W, DSA_W, DSA_W, IDX_HEADS * IDX_DIM, IDX_DIM, IDX_HEADS,
            POOL_W, POOL_W)

LANE = 128
SUBLANE = 8
VMEM_LIMIT_BYTES = 56 * 1024 * 1024

GLA_HP = GLA_HEADS * LANE
POOL_HALO = 16
ONES_ROWS = 16
LOG2E = math.log2(math.e)
DSA_Q_SCALE = (DSA_DH ** -0.5) * LOG2E

_NAT = {}
_off = 0
for _name, _w in (("gq", GLA_HP), ("gk", GLA_HP), ("gv", GLA_HP), ("gg", GLA_HP), ("gz", LANE),
                  ("dk", DSA_W), ("kx", LANE), ("dg", DSA_W), ("pu", POOL_W), ("pg", POOL_W)):
    _NAT[_name] = (_off, _off + _w)
    _off += _w
NAT_COLS = _off
_TR = {}
_off = 0
for _name, _w in (("dq", DSA_W), ("dv", DSA_W), ("qi", IDX_HEADS * IDX_DIM), ("wi", SUBLANE)):
    _TR[_name] = (_off, _off + _w)
    _off += _w
TR_ROWS = _off

INT_MIN = -2 ** 31
NEG_BIG = -1e30

f32 = jnp.float32
bf16 = jnp.bfloat16
i32 = jnp.int32


def _silu(x):
    return x * jax.nn.sigmoid(x)


def _dot(a, b):
    return jnp.dot(a, b, preferred_element_type=f32)


def _dot_nt(a, b):
    return lax.dot_general(a, b, (((1,), (1,)), ((), ())), preferred_element_type=f32)


def _dot_tn(a, b):
    return lax.dot_general(a, b, (((0,), (0,)), ((), ())), preferred_element_type=f32)


def _params():
    return pltpu.CompilerParams(dimension_semantics=("arbitrary", "arbitrary"),
                                vmem_limit_bytes=VMEM_LIMIT_BYTES)


def _in_proj_kernel(x_ref, g_ref, wn_ref, wt_ref,
                    gq_ref, gk_ref, gv_ref, gg_ref, gz_ref, dk_ref, kx_ref, dg_ref, pu_ref, pg_ref,
                    dqT_ref, dvT_ref, qiT_ref, wiT_ref):
    x = x_ref[...]
    h = x * lax.rsqrt(jnp.mean(x * x, axis=-1, keepdims=True) + EPS) * g_ref[...]
    hb = h.astype(bf16)
    for name, ref in (("gq", gq_ref), ("gk", gk_ref), ("gv", gv_ref), ("gg", gg_ref), ("gz", gz_ref),
                      ("dk", dk_ref), ("kx", kx_ref), ("dg", dg_ref), ("pu", pu_ref), ("pg", pg_ref)):
        lo, hi = _NAT[name]
        ref[...] = _dot(hb, wn_ref[:, lo:hi]).astype(ref.dtype)
    for name, ref in (("dq", dqT_ref), ("dv", dvT_ref), ("qi", qiT_ref), ("wi", wiT_ref)):
        lo, hi = _TR[name]
        y = _dot_nt(wt_ref[lo:hi, :], hb)
        if name == "dq":
            y = y * DSA_Q_SCALE
        ref[...] = y.astype(ref.dtype)


def _in_proj(x, g, wn, wt, tm):
    B, T, D = x.shape
    nat_dtypes = {"gq": f32, "gk": f32, "gv": f32, "gg": f32, "gz": f32,
                  "dk": bf16, "kx": bf16, "dg": f32, "pu": f32, "pg": f32}
    tr_dtypes = {"dq": bf16, "dv": bf16, "qi": bf16, "wi": f32}
    out_shape, out_specs = [], []
    for name, dt in nat_dtypes.items():
        w = _NAT[name][1] - _NAT[name][0]
        out_shape.append(jax.ShapeDtypeStruct((B, T, w), dt))
        out_specs.append(pl.BlockSpec((None, tm, w), lambda b, t: (b, t, 0)))
    for name, dt in tr_dtypes.items():
        w = _TR[name][1] - _TR[name][0]
        out_shape.append(jax.ShapeDtypeStruct((B, w, T), dt))
        out_specs.append(pl.BlockSpec((None, w, tm), lambda b, t: (b, 0, t)))
    return pl.pallas_call(
        _in_proj_kernel,
        grid=(B, T // tm),
        in_specs=[pl.BlockSpec((None, tm, D), lambda b, t: (b, t, 0)),
                  pl.BlockSpec((1, D), lambda b, t: (0, 0)),
                  pl.BlockSpec((D, NAT_COLS), lambda b, t: (0, 0)),
                  pl.BlockSpec((TR_ROWS, D), lambda b, t: (0, 0))],
        out_specs=out_specs,
        out_shape=out_shape,
        compiler_params=_params(),
        name="in_proj",
    )(x, g, wn, wt)


def _gla_kernel(gq_ref, gk_ref, gv_ref, gg_ref, gz_ref, w2_ref, b2_ref, gn_ref, out_ref,
                st_ref, glog_ref):
    C = GLA_CHUNK
    n_chunks = gq_ref.shape[0] // C

    @pl.when(pl.program_id(1) == 0)
    def _():
        st_ref[...] = jnp.zeros_like(st_ref)

    z = _dot(gz_ref[...].astype(bf16), w2_ref[...]) + b2_ref[...]
    b = (jnp.minimum(z, 0.0) - jnp.log1p(jnp.exp(-jnp.abs(z)))) / GLA_GATE_NORM
    pos = lax.broadcasted_iota(i32, b.shape, 0) & (C - 1)
    shift = 1
    while shift < C:
        b = b + jnp.where(pos >= shift, pltpu.roll(b, shift, 0), 0.0)
        shift *= 2
    glog_ref[...] = b

    causal = lax.broadcasted_iota(i32, (C, C), 1) <= lax.broadcasted_iota(i32, (C, C), 0)
    gn = gn_ref[...]
    heads = range(GLA_HEADS)
    cols = [slice(h * LANE, (h + 1) * LANE) for h in heads]

    def chunk(c, carry):
        r = pl.ds(pl.multiple_of(c * C, C), C)
        bs = [glog_ref[r, cs] for cs in cols]
        b_last = [bb[C - 1:C, :] for bb in bs]
        ks = [gk_ref[r, cs] for cs in cols]
        vs = [gv_ref[r, cs].astype(bf16) for cs in cols]
        qe = [(gq_ref[r, cols[h]] * jnp.exp(bs[h]) * (GLA_DK ** -0.5)).astype(bf16) for h in heads]
        ke = [(ks[h] * jnp.exp(-bs[h])).astype(bf16) for h in heads]
        kd = [(ks[h] * jnp.exp(b_last[h] - bs[h])).astype(bf16) for h in heads]
        st = [st_ref[h] for h in heads]
        a = [_dot_nt(qe[h], ke[h]) for h in heads]
        inter = [_dot_nt(qe[h], st[h].astype(bf16)) for h in heads]
        u = [_dot_tn(vs[h], kd[h]) for h in heads]
        o = [_dot(jnp.where(causal, a[h], 0.0).astype(bf16), vs[h]) + inter[h] for h in heads]
        for h in heads:
            st_ref[h] = st[h] * jnp.exp(b_last[h]) + u[h]
            ms = jnp.sum(o[h] * o[h], axis=-1, keepdims=True) * (1.0 / GLA_DV)
            y = o[h] * lax.rsqrt(ms + EPS) * gn * _silu(gg_ref[r, cols[h]])
            out_ref[r, cols[h]] = y.astype(out_ref.dtype)
        return carry

    lax.fori_loop(0, n_chunks, chunk, 0)


def _gla(gq, gk, gv, gg, gz, w2, b2, gn, tg):
    B, T, _ = gq.shape
    tok = lambda w: pl.BlockSpec((None, tg, w), lambda b, t: (b, t, 0))
    const = lambda s: pl.BlockSpec(s, lambda b, t: (0, 0))
    return pl.pallas_call(
        _gla_kernel,
        grid=(B, T // tg),
        in_specs=[tok(GLA_HP), tok(GLA_HP), tok(GLA_HP), tok(GLA_HP), tok(LANE),
                  const((LANE, GLA_HP)), const((1, GLA_HP)), const((1, LANE))],
        out_specs=tok(GLA_HP),
        out_shape=jax.ShapeDtypeStruct((B, T, GLA_HP), bf16),
        scratch_shapes=[pltpu.VMEM((GLA_HEADS, LANE, LANE), f32),
                        pltpu.VMEM((tg, GLA_HP), f32)],
        compiler_params=_params(),
        name="gla",
    )(gq, gk, gv, gg, gz, w2, b2, gn)


def _pool_kernel(pu_ref, pg_ref, w_ref, sc_ref, out_ref, halo_ref):
    tp = pu_ref.shape[0]
    H = POOL_HALO

    @pl.when(pl.program_id(1) == 0)
    def _():
        halo_ref[...] = jnp.zeros_like(halo_ref)

    u = pu_ref[...]
    ext = jnp.concatenate([halo_ref[...], u], axis=0)
    halo_ref[...] = u[tp - H:, :]
    s2 = ext + pltpu.roll(ext, 1, 0)
    s4 = s2 + pltpu.roll(s2, 2, 0)
    s8 = s4 + pltpu.roll(s4, 4, 0)
    s16 = s8 + pltpu.roll(s8, 8, 0)
    lane = lax.broadcasted_iota(i32, (tp, POOL_W), 1)
    grp = lane // POOL_GC
    sw = jnp.where(grp == 0, s2[H:], jnp.where(grp == 1, s4[H:], jnp.where(grp == 2, s8[H:], s16[H:])))
    win = jnp.where(grp == 0, POOL_WINDOWS[0],
                    jnp.where(grp == 1, POOL_WINDOWS[1],
                              jnp.where(grp == 2, POOL_WINDOWS[2], POOL_WINDOWS[3])))
    t = pl.program_id(1) * tp + lax.broadcasted_iota(i32, (tp, POOL_W), 0)
    cnt = jnp.minimum(t + 1, win).astype(f32)
    pooled = sw / cnt - u
    y = _dot(pooled.astype(bf16), w_ref[...]) * sc_ref[...]
    out_ref[...] = (y * _silu(pg_ref[...])).astype(out_ref.dtype)


def _pool(pu, pg, wbd, sc, tp):
    B, T, _ = pu.shape
    tok = pl.BlockSpec((None, tp, POOL_W), lambda b, t: (b, t, 0))
    return pl.pallas_call(
        _pool_kernel,
        grid=(B, T // tp),
        in_specs=[tok, tok,
                  pl.BlockSpec((POOL_W, POOL_W), lambda b, t: (0, 0)),
                  pl.BlockSpec((1, POOL_W), lambda b, t: (0, 0))],
        out_specs=tok,
        out_shape=jax.ShapeDtypeStruct((B, T, POOL_W), bf16),
        scratch_shapes=[pltpu.VMEM((POOL_HALO, POOL_W), f32)],
        compiler_params=_params(),
        name="pool",
    )(pu, pg, wbd, sc)


def _dsa_kernel(qT_ref, qiT_ref, wiT_ref, k_ref, kx_ref, vT_ref, dg_ref, bias_ref, out_ref,
                keys_ref, planes_ref, active_ref, qm_ref, qim_ref, acc_ref, *, topk):
    tq = qT_ref.shape[1]
    tk = tq
    qb = pl.program_id(1)
    n_blocks = qb + 1

    def rows(kb):
        return pl.ds(pl.multiple_of(kb * tk, tk), tk)

    prow = lax.broadcasted_iota(i32, (LANE, tq), 0)
    for h in range(DSA_HEADS):
        pair = qT_ref[(h // 2) * LANE:(h // 2 + 1) * LANE, :]
        qm_ref[h // 2, :, (h % 2) * tq:(h % 2 + 1) * tq] = jnp.where(
            (prow // DSA_DH) == (h % 2), pair, jnp.zeros_like(pair))
    qi = qiT_ref[...]
    for h in range(IDX_HEADS):
        qim_ref[:, h * tq:(h + 1) * tq] = jnp.where((prow // IDX_DIM) == h, qi, jnp.zeros_like(qi))
    w = wiT_ref[...] * ((IDX_DIM ** -0.5) * (IDX_HEADS ** -0.5))

    def score_keys(r):
        d = _dot(kx_ref[r, :], qim_ref[...])
        s = jnp.zeros((d.shape[0], tq), f32)
        for h in range(IDX_HEADS):
            s = s + jnp.maximum(d[:, h * tq:(h + 1) * tq], 0.0) * w[h:h + 1, :]
        bits = lax.bitcast_convert_type(s, i32)
        return jnp.where(bits < 0, INT_MIN - bits, bits)

    def store_keys(kb, keys):
        keys_ref[rows(kb), :] = keys
        u = keys ^ INT_MIN
        words = [u[SUBLANE * k:SUBLANE * (k + 1), :] for k in range(32)]
        j, m = 16, 0x0000FFFF
        while j:
            for k in range(32):
                if k & j == 0:
                    t = (lax.shift_right_logical(words[k], jnp.int32(j)) ^ words[k + j]) & m
                    words[k + j] = words[k + j] ^ t
                    words[k] = words[k] ^ lax.shift_left(t, jnp.int32(j))
            j >>= 1
            m ^= m << j
        r8 = pl.ds(pl.multiple_of(kb * SUBLANE, SUBLANE), SUBLANE)
        for b in range(32):
            planes_ref[b, r8, :] = words[b]

    def score_body(i, c, span):
        keys = score_keys(pl.ds(pl.multiple_of(i * span, span), span))
        for part in range(span // tk):
            store_keys(i * (span // tk) + part, keys[part * tk:(part + 1) * tk, :])
        return c

    @pl.when((pl.program_id(0) == 0) & (qb == 0))
    def _():
        planes_ref[...] = jnp.zeros_like(planes_ref)

    lax.fori_loop(0, qb // 2, functools.partial(score_body, span=2 * tk), 0)
    lax.fori_loop(2 * (qb // 2), qb, functools.partial(score_body, span=tk), 0)
    krow = lax.broadcasted_iota(i32, (tk, tq), 0)
    qcol = lax.broadcasted_iota(i32, (tk, tq), 1)
    store_keys(qb, jnp.where(krow <= qcol, score_keys(rows(qb)), INT_MIN))

    @pl.when(n_blocks % 2 == 1)
    def _():
        keys_ref[rows(n_blocks), :] = jnp.full((tk, tq), INT_MIN, i32)

    n_pairs = (n_blocks + 1) // 2

    n_rows = planes_ref.shape[1]
    prow8 = lax.broadcasted_iota(i32, (n_rows, tq), 0)
    active_ref[...] = jnp.where(prow8 < n_blocks * SUBLANE, -1, 0)

    def radix_step(i, carry):
        n_gt, tau_u = carry
        b = 31 - i
        plane = planes_ref[b]
        ones = lax.population_count(active_ref[...] & plane)
        n_set = jnp.sum(jnp.sum(ones.reshape(n_rows // SUBLANE, SUBLANE, tq), axis=0), axis=0, keepdims=True)
        take = n_gt + n_set >= topk
        active_ref[...] = active_ref[...] & (plane ^ jnp.where(take, 0, -1))
        return (jnp.where(take, n_gt, n_gt + n_set),
                tau_u | jnp.where(take, lax.shift_left(jnp.int32(1), b), 0))

    n_gt, tau_u = lax.fori_loop(0, 32, radix_step, (jnp.zeros((1, tq), i32), jnp.zeros((1, tq), i32)))
    tau = tau_u ^ INT_MIN
    need = jnp.where(tau == INT_MIN, 0, topk - n_gt).astype(f32)

    tri = (lax.broadcasted_iota(i32, (tk, tk), 1) <= lax.broadcasted_iota(i32, (tk, tk), 0)).astype(bf16)

    def mask_body(p, run):
        ks = [keys_ref[rows(2 * p + half), :] for half in range(2)]
        eqs = [k == tau for k in ks]
        cnt = [_dot(tri, e.astype(bf16)) for e in eqs]
        for half in range(2):
            pre = cnt[half] + run
            sel = (ks[half] > tau) | (eqs[half] & (pre <= need))
            keys_ref[rows(2 * p + half), :] = lax.bitcast_convert_type(
                jnp.where(sel, 0.0, NEG_BIG).astype(f32), i32)
            run = pre[tk - 1:tk, :]
        return run

    lax.fori_loop(0, n_pairs, mask_body, jnp.zeros((1, tq), f32))

    acc_ref[...] = jnp.zeros_like(acc_ref)

    def attend(i, ms, span, near):
        r = pl.ds(pl.multiple_of(i * span, span), span)
        ones = jnp.ones((ONES_ROWS, span), bf16)
        madd = lax.bitcast_convert_type(keys_ref[r, :], f32)
        dots = [_dot(k_ref[r, g * LANE:(g + 1) * LANE], qm_ref[g]) for g in range(DSA_HEADS // 2)]
        new_ms = []
        for h in range(DSA_HEADS):
            lg = dots[h // 2][:, (h % 2) * tq:(h % 2 + 1) * tq] + madd
            if near:
                j = i - (qb - 2)
                lg = lg + bias_ref[h, pl.ds(pl.multiple_of(j * tk, tk), tk), :]
            m_new = jnp.maximum(ms[h], jnp.max(lg, axis=0, keepdims=True))
            alpha = jnp.exp2(ms[h] - m_new)
            p = jnp.exp2(lg - m_new).astype(bf16)
            v1 = jnp.concatenate([vT_ref[h * DSA_DH:(h + 1) * DSA_DH, r], ones], axis=0)
            acc_ref[h] = alpha * acc_ref[h] + _dot(v1, p)
            new_ms.append(m_new)
        return tuple(new_ms)

    n_far_pairs = jnp.maximum(qb - 1, 0) // 2
    ms = tuple(jnp.full((1, tq), NEG_BIG, f32) for _ in range(DSA_HEADS))
    ms = lax.fori_loop(0, n_far_pairs, functools.partial(attend, span=2 * tk, near=False), ms)
    lax.fori_loop(2 * n_far_pairs, n_blocks, functools.partial(attend, span=tk, near=True), ms)
    oT = jnp.concatenate([acc_ref[h, :DSA_DH, :] / acc_ref[h, DSA_DH:DSA_DH + 1, :]
                          for h in range(DSA_HEADS)], axis=0)
    out_ref[...] = (oT.T * _silu(dg_ref[...])).astype(out_ref.dtype)


def _dsa(dqT, qiT, wiT, dk, kx, dvT, dg, bias, tq, topk):
    B, T, _ = dk.shape
    whole = lambda shape, imap: pl.BlockSpec(shape, imap, pipeline_mode=pl.Buffered(1))
    return pl.pallas_call(
        functools.partial(_dsa_kernel, topk=topk),
        grid=(B, T // tq),
        in_specs=[pl.BlockSpec((None, DSA_W, tq), lambda b, q: (b, 0, q)),
                  pl.BlockSpec((None, IDX_HEADS * IDX_DIM, tq), lambda b, q: (b, 0, q)),
                  pl.BlockSpec((None, SUBLANE, tq), lambda b, q: (b, 0, q)),
                  whole((None, T, DSA_W), lambda b, q: (b, 0, 0)),
                  whole((None, T, LANE), lambda b, q: (b, 0, 0)),
                  whole((None, DSA_W, T), lambda b, q: (b, 0, 0)),
                  pl.BlockSpec((None, tq, DSA_W), lambda b, q: (b, q, 0)),
                  whole((DSA_HEADS, 3 * tq, tq), lambda b, q: (0, 0, 0))],
        out_specs=pl.BlockSpec((None, tq, DSA_W), lambda b, q: (b, q, 0)),
        out_shape=jax.ShapeDtypeStruct((B, T, DSA_W), bf16),
        scratch_shapes=[pltpu.VMEM((T + tq, tq), i32),
                        pltpu.VMEM((32, T // 32, tq), i32),
                        pltpu.VMEM((T // 32, tq), i32),
                        pltpu.VMEM((DSA_HEADS // 2, LANE, 2 * tq), bf16),
                        pltpu.VMEM((LANE, IDX_HEADS * tq), bf16),
                        pltpu.VMEM((DSA_HEADS, DSA_DH + ONES_ROWS, tq), f32)],
        compiler_params=_params(),
        name="dsa",
    )(dqT, qiT, wiT, dk, kx, dvT, dg, bias)


def _out_proj_kernel(x_ref, yg_ref, yd_ref, yp_ref, wg_ref, wd_ref, wp_ref, fg_ref, out_ref, *, final):
    x = x_ref[...] + _dot(yg_ref[...], wg_ref[...]) + _dot(yd_ref[...], wd_ref[...]) \
        + _dot(yp_ref[...], wp_ref[...])
    if final:
        x = x * lax.rsqrt(jnp.mean(x * x, axis=-1, keepdims=True) + EPS) * fg_ref[...]
    out_ref[...] = x


def _out_proj(x, yg, yd, yp, wg, wd, wp, fg, tm, final):
    B, T, D = x.shape
    tok = lambda w: pl.BlockSpec((None, tm, w), lambda b, t: (b, t, 0))
    const = lambda s: pl.BlockSpec(s, lambda b, t: (0, 0))
    return pl.pallas_call(
        functools.partial(_out_proj_kernel, final=final),
        grid=(B, T // tm),
        in_specs=[tok(D), tok(GLA_HP), tok(DSA_W), tok(POOL_W),
                  const((GLA_HP, D)), const((DSA_W, D)), const((POOL_W, D)), const((1, D))],
        out_specs=tok(D),
        out_shape=jax.ShapeDtypeStruct((B, T, D), f32),
        compiler_params=_params(),
        name="out_proj_final" if final else "out_proj",
    )(x, yg, yd, yp, wg, wd, wp, fg)


def _pad_heads(w, heads, d):
    lead = w.shape[:-1]
    w = w.reshape(*lead, heads, d)
    w = jnp.pad(w, [(0, 0)] * len(lead) + [(0, 0), (0, LANE - d)])
    return w.reshape(*lead, heads * LANE)


def _split_cols(w):
    outs, off = [], 0
    for s in IN_SIZES:
        outs.append(w[..., off:off + s])
        off += s
    return outs


def _layer_weights(w_in, gate_w2, gate_b, gla_norm_g, pool_w, pool_scale, w_out):
    (gq, gk, gv, gz, gg, dq, dk, dv, dg, qi, ki, wi, pu, pg) = _split_cols(w_in)
    D = w_in.shape[0]
    wn = jnp.concatenate([
        _pad_heads(gq, GLA_HEADS, GLA_DK), _pad_heads(gk, GLA_HEADS, GLA_DK),
        _pad_heads(gv, GLA_HEADS, GLA_DV), _pad_heads(gg, GLA_HEADS, GLA_DV),
        jnp.pad(gz, ((0, 0), (0, LANE - GLA_GATE_RANK))),
        dk, jnp.tile(ki, (1, IDX_HEADS)), dg, pu, pg], axis=1).astype(bf16)
    wt = jnp.concatenate([
        dq, dv, qi, jnp.pad(wi, ((0, 0), (0, SUBLANE - IDX_HEADS)))], axis=1).T.astype(bf16)
    w2 = jnp.pad(_pad_heads(gate_w2, GLA_HEADS, GLA_DK), ((0, LANE - GLA_GATE_RANK), (0, 0))).astype(bf16)
    b2 = _pad_heads(gate_b[None, :], GLA_HEADS, GLA_DK)
    gn = jnp.pad(gla_norm_g[None, :], ((0, 0), (0, LANE - GLA_DV)))
    wbd = jax.scipy.linalg.block_diag(*[pool_w[g] for g in range(POOL_GROUPS)]).astype(bf16)
    sc = pool_scale[None, :]
    wg = _pad_heads(w_out[:GLA_W].T, GLA_HEADS, GLA_DV).T.astype(bf16)
    wd = w_out[GLA_W:GLA_W + DSA_W].astype(bf16)
    wp = w_out[GLA_W + DSA_W:].astype(bf16)
    return wn, wt, w2, b2, gn, wbd, sc, wg, wd, wp


def _t5_bucket_table(n):
    rel = np.arange(n)
    max_exact = REL_BUCKETS // 2
    relf = np.maximum(rel, 1).astype(np.float32)
    large = max_exact + (np.log(relf / np.float32(max_exact)) / np.float32(math.log(REL_MAX_DIST / max_exact))
                         * np.float32(REL_BUCKETS - max_exact)).astype(np.int32)
    large = np.minimum(large, REL_BUCKETS - 1)
    return np.where(rel < max_exact, rel, large)


def _near_bias(rel_bias, tq):
    bucket = _t5_bucket_table(3 * tq)
    assert np.all(bucket[tq + 1:] == REL_BUCKETS - 1)
    by_rel = (rel_bias[bucket] - rel_bias[REL_BUCKETS - 1][None, :]) * LOG2E
    n = 4 * tq - 1
    g = by_rel[np.maximum(np.arange(n) - (tq - 1), 0)].T
    reps = -(-(3 * tq * (n + 1)) // n)
    tiled = jnp.broadcast_to(g[:, None, :], (g.shape[0], reps, n)).reshape(g.shape[0], reps * n)
    skew = tiled[:, :3 * tq * (n + 1)].reshape(-1, 3 * tq, n + 1)
    return skew[:, ::-1, :tq]


def kernel(x, norm_g, w_in, gla_gate_w2, gla_gate_b, gla_norm_g, rel_bias, pool_w, pool_scale, w_out,
           final_norm_g):
    B, T, D = x.shape
    assert D == D_MODEL
    tm = min(512, T)
    tq = min(256, T)
    topk = min(TOPK_MAX, T // 4)
    assert T % tm == 0 and T % (2 * tq) == 0 and tq >= LANE and T % GLA_CHUNK == 0
    bias = _near_bias(rel_bias.astype(f32), tq)
    fg = final_norm_g[None, :]
    for l in range(DEPTH):
        wn, wt, w2, b2, gn, wbd, sc, wg, wd, wp = _layer_weights(
            w_in[l], gla_gate_w2[l], gla_gate_b[l], gla_norm_g[l], pool_w[l], pool_scale[l], w_out[l])
        (gq, gk, gv, gg, gz, dk, kx, dg, pu, pg, dqT, dvT, qiT, wiT) = _in_proj(x, norm_g[l][None, :], wn, wt, tm)
        yg = _gla(gq, gk, gv, gg, gz, w2, b2, gn, tm)
        yp = _pool(pu, pg, wbd, sc, tm)
        yd = _dsa(dqT, qiT, wiT, dk, kx, dvT, dg, bias, tq, topk)
        x = _out_proj(x, yg, yd, yp, wg, wd, wp, fg, tm, final=(l == DEPTH - 1))
    return x
```

```python
import functools
import math

import numpy as np
import jax
import jax.numpy as jnp
from jax import lax
from jax.experimental import pallas as pl
from jax.experimental.pallas import tpu as pltpu

D_MODEL = 1024
DEPTH = 4
EPS = 1e-6
GLA_HEADS = 4
GLA_DK = 48
GLA_DV = 96
GLA_GATE_RANK = 16
GLA_GATE_NORM = 16.0
GLA_CHUNK = 64
DSA_HEADS = 6
DSA_DH = 64
IDX_HEADS = 4
IDX_DIM = 32
TOPK_MAX = 256
POOL_GROUPS = 4
POOL_GC = 64
POOL_WINDOWS = (2, 4, 8, 16)
REL_BUCKETS = 32
REL_MAX_DIST = 128

GLA_W = GLA_HEADS * GLA_DV
DSA_W = DSA_HEADS * DSA_DH
POOL_W = POOL_GROUPS * POOL_GC
IN_SIZES = (GLA_HEADS * GLA_DK, GLA_HEADS * GLA_DK, GLA_W, GLA_GATE_RANK, GLA_W,
            DSA_W, DSA_W, DSA_W, DSA_W, IDX_HEADS * IDX_DIM, IDX_DIM, IDX_HEADS,
            POOL_W, POOL_W)

LANE = 128
SUBLANE = 8
VMEM_LIMIT_BYTES = 56 * 1024 * 1024

GLA_HP = GLA_HEADS * LANE
POOL_HALO = 16
ONES_ROWS = 16
LOG2E = math.log2(math.e)
DSA_Q_SCALE = (DSA_DH ** -0.5) * LOG2E
SHIFT_MARGIN = 1.01
MIN_DENOM = 2.0 ** -60

_NAT = {}
_off = 0
for _name, _w in (("gq", GLA_HP), ("gk", GLA_HP), ("gv", GLA_HP), ("gg", GLA_HP), ("gz", LANE),
                  ("dk", DSA_W), ("kx", LANE), ("dg", DSA_W), ("pu", POOL_W), ("pg", POOL_W)):
    _NAT[_name] = (_off, _off + _w)
    _off += _w
NAT_COLS = _off
_TR = {}
_off = 0
for _name, _w in (("dq", DSA_W), ("dv", DSA_W), ("qi", IDX_HEADS * IDX_DIM), ("wi", SUBLANE)):
    _TR[_name] = (_off, _off + _w)
    _off += _w
TR_ROWS = _off

INT_MIN = -2 ** 31
NEG_BIG = -1e30

f32 = jnp.float32
bf16 = jnp.bfloat16
i32 = jnp.int32


def _silu(x):
    return x * jax.nn.sigmoid(x)


def _dot(a, b):
    return jnp.dot(a, b, preferred_element_type=f32)


def _dot_nt(a, b):
    return lax.dot_general(a, b, (((1,), (1,)), ((), ())), preferred_element_type=f32)


def _dot_tn(a, b):
    return lax.dot_general(a, b, (((0,), (0,)), ((), ())), preferred_element_type=f32)


def _params():
    return pltpu.CompilerParams(dimension_semantics=("arbitrary", "arbitrary"),
                                vmem_limit_bytes=VMEM_LIMIT_BYTES)


def _in_proj_kernel(x_ref, g_ref, wn_ref, wt_ref,
                    gq_ref, gk_ref, gv_ref, gg_ref, gz_ref, dk_ref, kx_ref, dg_ref, pu_ref, pg_ref,
                    dqT_ref, dvT_ref, qiT_ref, wiT_ref):
    x = x_ref[...]
    h = x * lax.rsqrt(jnp.mean(x * x, axis=-1, keepdims=True) + EPS) * g_ref[...]
    hb = h.astype(bf16)
    for name, ref in (("gq", gq_ref), ("gk", gk_ref), ("gv", gv_ref), ("gg", gg_ref), ("gz", gz_ref),
                      ("dk", dk_ref), ("kx", kx_ref), ("dg", dg_ref), ("pu", pu_ref), ("pg", pg_ref)):
        lo, hi = _NAT[name]
        ref[...] = _dot(hb, wn_ref[:, lo:hi]).astype(ref.dtype)
    for name, ref in (("dq", dqT_ref), ("dv", dvT_ref), ("qi", qiT_ref), ("wi", wiT_ref)):
        lo, hi = _TR[name]
        y = _dot_nt(wt_ref[lo:hi, :], hb)
        if name == "dq":
            y = y * DSA_Q_SCALE
        ref[...] = y.astype(ref.dtype)


def _in_proj(x, g, wn, wt, tm):
    B, T, D = x.shape
    nat_dtypes = {"gq": f32, "gk": f32, "gv": f32, "gg": f32, "gz": f32,
                  "dk": bf16, "kx": bf16, "dg": f32, "pu": f32, "pg": f32}
    tr_dtypes = {"dq": bf16, "dv": bf16, "qi": bf16, "wi": f32}
    out_shape, out_specs = [], []
    for name, dt in nat_dtypes.items():
        w = _NAT[name][1] - _NAT[name][0]
        out_shape.append(jax.ShapeDtypeStruct((B, T, w), dt))
        out_specs.append(pl.BlockSpec((None, tm, w), lambda b, t: (b, t, 0)))
    for name, dt in tr_dtypes.items():
        w = _TR[name][1] - _TR[name][0]
        out_shape.append(jax.ShapeDtypeStruct((B, w, T), dt))
        out_specs.append(pl.BlockSpec((None, w, tm), lambda b, t: (b, 0, t)))
    return pl.pallas_call(
        _in_proj_kernel,
        grid=(B, T // tm),
        in_specs=[pl.BlockSpec((None, tm, D), lambda b, t: (b, t, 0)),
                  pl.BlockSpec((1, D), lambda b, t: (0, 0)),
                  pl.BlockSpec((D, NAT_COLS), lambda b, t: (0, 0)),
                  pl.BlockSpec((TR_ROWS, D), lambda b, t: (0, 0))],
        out_specs=out_specs,
        out_shape=out_shape,
        compiler_params=_params(),
        name="in_proj",
    )(x, g, wn, wt)


def _gla_kernel(gq_ref, gk_ref, gv_ref, gg_ref, gz_ref, w2_ref, b2_ref, gn_ref, out_ref,
                st_ref, glog_ref):
    C = GLA_CHUNK
    n_chunks = gq_ref.shape[0] // C

    @pl.when(pl.program_id(1) == 0)
    def _():
        st_ref[...] = jnp.zeros_like(st_ref)

    z = _dot(gz_ref[...].astype(bf16), w2_ref[...]) + b2_ref[...]
    b = (jnp.minimum(z, 0.0) - jnp.log1p(jnp.exp(-jnp.abs(z)))) / GLA_GATE_NORM
    pos = lax.broadcasted_iota(i32, b.shape, 0) & (C - 1)
    shift = 1
    while shift < C:
        b = b + jnp.where(pos >= shift, pltpu.roll(b, shift, 0), 0.0)
        shift *= 2
    glog_ref[...] = b

    causal = lax.broadcasted_iota(i32, (C, C), 1) <= lax.broadcasted_iota(i32, (C, C), 0)
    gn = gn_ref[...]
    heads = range(GLA_HEADS)
    cols = [slice(h * LANE, (h + 1) * LANE) for h in heads]

    def chunk(c, carry):
        r = pl.ds(pl.multiple_of(c * C, C), C)
        bs = [glog_ref[r, cs] for cs in cols]
        b_last = [bb[C - 1:C, :] for bb in bs]
        ks = [gk_ref[r, cs] for cs in cols]
        vs = [gv_ref[r, cs].astype(bf16) for cs in cols]
        qe = [(gq_ref[r, cols[h]] * jnp.exp(bs[h]) * (GLA_DK ** -0.5)).astype(bf16) for h in heads]
        ke = [(ks[h] * jnp.exp(-bs[h])).astype(bf16) for h in heads]
        kd = [(ks[h] * jnp.exp(b_last[h] - bs[h])).astype(bf16) for h in heads]
        st = [st_ref[h] for h in heads]
        a = [_dot_nt(qe[h], ke[h]) for h in heads]
        inter = [_dot_nt(qe[h], st[h].astype(bf16)) for h in heads]
        u = [_dot_tn(vs[h], kd[h]) for h in heads]
        o = [_dot(jnp.where(causal, a[h], 0.0).astype(bf16), vs[h]) + inter[h] for h in heads]
        for h in heads:
            st_ref[h] = st[h] * jnp.exp(b_last[h]) + u[h]
            ms = jnp.sum(o[h] * o[h], axis=-1, keepdims=True) * (1.0 / GLA_DV)
            y = o[h] * lax.rsqrt(ms + EPS) * gn * _silu(gg_ref[r, cols[h]])
            out_ref[r, cols[h]] = y.astype(out_ref.dtype)
        return carry

    lax.fori_loop(0, n_chunks, chunk, 0)


def _gla(gq, gk, gv, gg, gz, w2, b2, gn, tg):
    B, T, _ = gq.shape
    tok = lambda w: pl.BlockSpec((None, tg, w), lambda b, t: (b, t, 0))
    const = lambda s: pl.BlockSpec(s, lambda b, t: (0, 0))
    return pl.pallas_call(
        _gla_kernel,
        grid=(B, T // tg),
        in_specs=[tok(GLA_HP), tok(GLA_HP), tok(GLA_HP), tok(GLA_HP), tok(LANE),
                  const((LANE, GLA_HP)), const((1, GLA_HP)), const((1, LANE))],
        out_specs=tok(GLA_HP),
        out_shape=jax.ShapeDtypeStruct((B, T, GLA_HP), bf16),
        scratch_shapes=[pltpu.VMEM((GLA_HEADS, LANE, LANE), f32),
                        pltpu.VMEM((tg, GLA_HP), f32)],
        compiler_params=_params(),
        name="gla",
    )(gq, gk, gv, gg, gz, w2, b2, gn)


def _pool_kernel(pu_ref, pg_ref, w_ref, sc_ref, out_ref, halo_ref):
    tp = pu_ref.shape[0]
    H = POOL_HALO

    @pl.when(pl.program_id(1) == 0)
    def _():
        halo_ref[...] = jnp.zeros_like(halo_ref)

    u = pu_ref[...]
    ext = jnp.concatenate([halo_ref[...], u], axis=0)
    halo_ref[...] = u[tp - H:, :]
    s2 = ext + pltpu.roll(ext, 1, 0)
    s4 = s2 + pltpu.roll(s2, 2, 0)
    s8 = s4 + pltpu.roll(s4, 4, 0)
    s16 = s8 + pltpu.roll(s8, 8, 0)
    lane = lax.broadcasted_iota(i32, (tp, POOL_W), 1)
    grp = lane // POOL_GC
    sw = jnp.where(grp == 0, s2[H:], jnp.where(grp == 1, s4[H:], jnp.where(grp == 2, s8[H:], s16[H:])))
    win = jnp.where(grp == 0, POOL_WINDOWS[0],
                    jnp.where(grp == 1, POOL_WINDOWS[1],
                              jnp.where(grp == 2, POOL_WINDOWS[2], POOL_WINDOWS[3])))
    t = pl.program_id(1) * tp + lax.broadcasted_iota(i32, (tp, POOL_W), 0)
    cnt = jnp.minimum(t + 1, win).astype(f32)
    pooled = sw / cnt - u
    y = _dot(pooled.astype(bf16), w_ref[...]) * sc_ref[...]
    out_ref[...] = (y * _silu(pg_ref[...])).astype(out_ref.dtype)


def _pool(pu, pg, wbd, sc, tp):
    B, T, _ = pu.shape
    tok = pl.BlockSpec((None, tp, POOL_W), lambda b, t: (b, t, 0))
    return pl.pallas_call(
        _pool_kernel,
        grid=(B, T // tp),
        in_specs=[tok, tok,
                  pl.BlockSpec((POOL_W, POOL_W), lambda b, t: (0, 0)),
                  pl.BlockSpec((1, POOL_W), lambda b, t: (0, 0))],
        out_specs=tok,
        out_shape=jax.ShapeDtypeStruct((B, T, POOL_W), bf16),
        scratch_shapes=[pltpu.VMEM((POOL_HALO, POOL_W), f32)],
        compiler_params=_params(),
        name="pool",
    )(pu, pg, wbd, sc)


def _dsa_kernel(qT_ref, qiT_ref, wiT_ref, k_ref, kx_ref, vT_ref, dg_ref, bias_ref, bmax_ref, out_ref,
                keys_ref, planes_ref, active_ref, qm_ref, qim_ref, acc_ref, kmax_ref, *, topk):
    tq = qT_ref.shape[1]
    tk = tq
    qb = pl.program_id(1)
    n_blocks = qb + 1

    def rows(kb):
        return pl.ds(pl.multiple_of(kb * tk, tk), tk)

    @pl.when(qb == 0)
    def _():
        def body(kb, mx):
            kk = k_ref[rows(kb), :].astype(f32)
            n2 = jnp.sum(kk * kk, axis=1, keepdims=True)
            return jnp.maximum(mx, jnp.max(n2, axis=0, keepdims=True))
        mx = lax.fori_loop(0, k_ref.shape[0] // tk, body, jnp.zeros((1, 1), f32))
        kmax_ref[...] = jnp.broadcast_to(jnp.sqrt(mx), kmax_ref.shape)

    prow = lax.broadcasted_iota(i32, (LANE, tq), 0)
    for h in range(DSA_HEADS):
        pair = qT_ref[(h // 2) * LANE:(h // 2 + 1) * LANE, :]
        qm_ref[h // 2, :, (h % 2) * tq:(h % 2 + 1) * tq] = jnp.where(
            (prow // DSA_DH) == (h % 2), pair, jnp.zeros_like(pair))
    qi = qiT_ref[...]
    for h in range(IDX_HEADS):
        qim_ref[:, h * tq:(h + 1) * tq] = jnp.where((prow // IDX_DIM) == h, qi, jnp.zeros_like(qi))
    w = wiT_ref[...] * ((IDX_DIM ** -0.5) * (IDX_HEADS ** -0.5))

    def score_keys(r):
        d = _dot(kx_ref[r, :], qim_ref[...])
        s = jnp.zeros((d.shape[0], tq), f32)
        for h in range(IDX_HEADS):
            s = s + jnp.maximum(d[:, h * tq:(h + 1) * tq], 0.0) * w[h:h + 1, :]
        bits = lax.bitcast_convert_type(s, i32)
        return jnp.where(bits < 0, INT_MIN - bits, bits)

    def store_keys(kb, keys):
        keys_ref[rows(kb), :] = keys
        u = keys ^ INT_MIN
        words = [u[SUBLANE * k:SUBLANE * (k + 1), :] for k in range(32)]
        j, m = 16, 0x0000FFFF
        while j:
            for k in range(32):
                if k & j == 0:
                    t = (lax.shift_right_logical(words[k], jnp.int32(j)) ^ words[k + j]) & m
                    words[k + j] = words[k + j] ^ t
                    words[k] = words[k] ^ lax.shift_left(t, jnp.int32(j))
            j >>= 1
            m ^= m << j
        r8 = pl.ds(pl.multiple_of(kb * SUBLANE, SUBLANE), SUBLANE)
        for b in range(32):
            planes_ref[b, r8, :] = words[b]

    def score_body(i, c, span):
        keys = score_keys(pl.ds(pl.multiple_of(i * span, span), span))
        for part in range(span // tk):
            store_keys(i * (span // tk) + part, keys[part * tk:(part + 1) * tk, :])
        return c

    @pl.when((pl.program_id(0) == 0) & (qb == 0))
    def _():
        planes_ref[...] = jnp.zeros_like(planes_ref)

    lax.fori_loop(0, qb // 2, functools.partial(score_body, span=2 * tk), 0)
    lax.fori_loop(2 * (qb // 2), qb, functools.partial(score_body, span=tk), 0)
    krow = lax.broadcasted_iota(i32, (tk, tq), 0)
    qcol = lax.broadcasted_iota(i32, (tk, tq), 1)
    store_keys(qb, jnp.where(krow <= qcol, score_keys(rows(qb)), INT_MIN))

    @pl.when(n_blocks % 2 == 1)
    def _():
        keys_ref[rows(n_blocks), :] = jnp.full((tk, tq), INT_MIN, i32)

    n_pairs = (n_blocks + 1) // 2

    n_rows = planes_ref.shape[1]
    prow8 = lax.broadcasted_iota(i32, (n_rows, tq), 0)
    active_ref[...] = jnp.where(prow8 < n_blocks * SUBLANE, -1, 0)

    def radix_step(i, carry):
        n_gt, tau_u = carry
        b = 31 - i
        plane = planes_ref[b]
        ones = lax.population_count(active_ref[...] & plane)
        n_set = jnp.sum(jnp.sum(ones.reshape(n_rows // SUBLANE, SUBLANE, tq), axis=0), axis=0, keepdims=True)
        take = n_gt + n_set >= topk
        active_ref[...] = active_ref[...] & (plane ^ jnp.where(take, 0, -1))
        return (jnp.where(take, n_gt, n_gt + n_set),
                tau_u | jnp.where(take, lax.shift_left(jnp.int32(1), b), 0))

    n_gt, tau_u = lax.fori_loop(0, 32, radix_step, (jnp.zeros((1, tq), i32), jnp.zeros((1, tq), i32)))
    tau = tau_u ^ INT_MIN
    need = jnp.where(tau == INT_MIN, 0, topk - n_gt).astype(f32)

    tri = (lax.broadcasted_iota(i32, (tk, tk), 1) <= lax.broadcasted_iota(i32, (tk, tk), 0)).astype(bf16)

    def mask_body(p, run):
        ks = [keys_ref[rows(2 * p + half), :] for half in range(2)]
        eqs = [k == tau for k in ks]
        cnt = [_dot(tri, e.astype(bf16)) for e in eqs]
        for half in range(2):
            pre = cnt[half] + run
            sel = (ks[half] > tau) | (eqs[half] & (pre <= need))
            keys_ref[rows(2 * p + half), :] = lax.bitcast_convert_type(
                jnp.where(sel, 0.0, NEG_BIG).astype(f32), i32)
            run = pre[tk - 1:tk, :]
        return run

    lax.fori_loop(0, n_pairs, mask_body, jnp.zeros((1, tq), f32))

    def tile_ops(i, span, near):
        r = pl.ds(pl.multiple_of(i * span, span), span)
        ones = jnp.ones((ONES_ROWS, span), bf16)
        madd = lax.bitcast_convert_type(keys_ref[r, :], f32)
        dots = [_dot(k_ref[r, g * LANE:(g + 1) * LANE], qm_ref[g]) for g in range(DSA_HEADS // 2)]

        def logits(h):
            lg = dots[h // 2][:, (h % 2) * tq:(h % 2 + 1) * tq] + madd
            if near:
                j = i - (qb - 2)
                lg = lg + bias_ref[h, pl.ds(pl.multiple_of(j * tk, tk), tk), :]
            return lg

        def v1(h):
            return jnp.concatenate([vT_ref[h * DSA_DH:(h + 1) * DSA_DH, r], ones], axis=0)

        return logits, v1

    def attend_online(i, ms, span, near):
        logits, v1 = tile_ops(i, span, near)
        new_ms = []
        for h in range(DSA_HEADS):
            lg = logits(h)
            m_new = jnp.maximum(ms[h], jnp.max(lg, axis=0, keepdims=True))
            alpha = jnp.exp2(ms[h] - m_new)
            acc_ref[h] = alpha * acc_ref[h] + _dot(v1(h), jnp.exp2(lg - m_new).astype(bf16))
            new_ms.append(m_new)
        return tuple(new_ms)

    def attend_fixed(i, shift, span, near):
        logits, v1 = tile_ops(i, span, near)
        for h in range(DSA_HEADS):
            acc_ref[h] = acc_ref[h] + _dot(v1(h), jnp.exp2(logits(h) - shift[h]).astype(bf16))
        return shift

    def sweep(body, carry):
        n_far_pairs = jnp.maximum(qb - 1, 0) // 2
        acc_ref[...] = jnp.zeros_like(acc_ref)
        carry = lax.fori_loop(0, n_far_pairs, functools.partial(body, span=2 * tk, near=False), carry)
        return lax.fori_loop(2 * n_far_pairs, n_blocks, functools.partial(body, span=tk, near=True), carry)

    q_norm = []
    for g in range(DSA_HEADS // 2):
        qf = qm_ref[g].astype(f32)
        n2 = jnp.sum(qf * qf, axis=0, keepdims=True)
        q_norm += [jnp.sqrt(n2[:, :tq]), jnp.sqrt(n2[:, tq:])]
    shift = tuple(q_norm[h] * (kmax_ref[0:1, 0:1] * SHIFT_MARGIN) + bmax_ref[h:h + 1, 0:1]
                  for h in range(DSA_HEADS))
    sweep(attend_fixed, shift)
    denom = acc_ref[0, DSA_DH:DSA_DH + 1, :]
    for h in range(1, DSA_HEADS):
        denom = jnp.minimum(denom, acc_ref[h, DSA_DH:DSA_DH + 1, :])
    accurate = jnp.min(denom) >= MIN_DENOM

    @pl.when(jnp.logical_not(accurate))
    def _():
        sweep(attend_online, tuple(jnp.full((1, tq), NEG_BIG, f32) for _ in range(DSA_HEADS)))

    oT = jnp.concatenate([acc_ref[h, :DSA_DH, :] / acc_ref[h, DSA_DH:DSA_DH + 1, :]
                          for h in range(DSA_HEADS)], axis=0)
    out_ref[...] = (oT.T * _silu(dg_ref[...])).astype(out_ref.dtype)


def _dsa(dqT, qiT, wiT, dk, kx, dvT, dg, bias, bmax, tq, topk):
    B, T, _ = dk.shape
    whole = lambda shape, imap: pl.BlockSpec(shape, imap, pipeline_mode=pl.Buffered(1))
    return pl.pallas_call(
        functools.partial(_dsa_kernel, topk=topk),
        grid=(B, T // tq),
        in_specs=[pl.BlockSpec((None, DSA_W, tq), lambda b, q: (b, 0, q)),
                  pl.BlockSpec((None, IDX_HEADS * IDX_DIM, tq), lambda b, q: (b, 0, q)),
                  pl.BlockSpec((None, SUBLANE, tq), lambda b, q: (b, 0, q)),
                  whole((None, T, DSA_W), lambda b, q: (b, 0, 0)),
                  whole((None, T, LANE), lambda b, q: (b, 0, 0)),
                  whole((None, DSA_W, T), lambda b, q: (b, 0, 0)),
                  pl.BlockSpec((None, tq, DSA_W), lambda b, q: (b, q, 0)),
                  whole((DSA_HEADS, 3 * tq, tq), lambda b, q: (0, 0, 0)),
                  pl.BlockSpec((SUBLANE, LANE), lambda b, q: (0, 0))],
        out_specs=pl.BlockSpec((None, tq, DSA_W), lambda b, q: (b, q, 0)),
        out_shape=jax.ShapeDtypeStruct((B, T, DSA_W), bf16),
        scratch_shapes=[pltpu.VMEM((T + tq, tq), i32),
                        pltpu.VMEM((32, T // 32, tq), i32),
                        pltpu.VMEM((T // 32, tq), i32),
                        pltpu.VMEM((DSA_HEADS // 2, LANE, 2 * tq), bf16),
                        pltpu.VMEM((LANE, IDX_HEADS * tq), bf16),
                        pltpu.VMEM((DSA_HEADS, DSA_DH + ONES_ROWS, tq), f32),
                        pltpu.VMEM((SUBLANE, LANE), f32)],
        compiler_params=_params(),
        name="dsa",
    )(dqT, qiT, wiT, dk, kx, dvT, dg, bias, bmax)


def _out_proj_kernel(x_ref, yg_ref, yd_ref, yp_ref, wg_ref, wd_ref, wp_ref, fg_ref, out_ref, *, final):
    x = x_ref[...] + _dot(yg_ref[...], wg_ref[...]) + _dot(yd_ref[...], wd_ref[...]) \
        + _dot(yp_ref[...], wp_ref[...])
    if final:
        x = x * lax.rsqrt(jnp.mean(x * x, axis=-1, keepdims=True) + EPS) * fg_ref[...]
    out_ref[...] = x


def _out_proj(x, yg, yd, yp, wg, wd, wp, fg, tm, final):
    B, T, D = x.shape
    tok = lambda w: pl.BlockSpec((None, tm, w), lambda b, t: (b, t, 0))
    const = lambda s: pl.BlockSpec(s, lambda b, t: (0, 0))
    return pl.pallas_call(
        functools.partial(_out_proj_kernel, final=final),
        grid=(B, T // tm),
        in_specs=[tok(D), tok(GLA_HP), tok(DSA_W), tok(POOL_W),
                  const((GLA_HP, D)), const((DSA_W, D)), const((POOL_W, D)), const((1, D))],
        out_specs=tok(D),
        out_shape=jax.ShapeDtypeStruct((B, T, D), f32),
        compiler_params=_params(),
        name="out_proj_final" if final else "out_proj",
    )(x, yg, yd, yp, wg, wd, wp, fg)


def _pad_heads(w, heads, d):
    lead = w.shape[:-1]
    w = w.reshape(*lead, heads, d)
    w = jnp.pad(w, [(0, 0)] * len(lead) + [(0, 0), (0, LANE - d)])
    return w.reshape(*lead, heads * LANE)


def _split_cols(w):
    outs, off = [], 0
    for s in IN_SIZES:
        outs.append(w[..., off:off + s])
        off += s
    return outs


def _layer_weights(w_in, gate_w2, gate_b, gla_norm_g, pool_w, pool_scale, w_out):
    (gq, gk, gv, gz, gg, dq, dk, dv, dg, qi, ki, wi, pu, pg) = _split_cols(w_in)
    D = w_in.shape[0]
    wn = jnp.concatenate([
        _pad_heads(gq, GLA_HEADS, GLA_DK), _pad_heads(gk, GLA_HEADS, GLA_DK),
        _pad_heads(gv, GLA_HEADS, GLA_DV), _pad_heads(gg, GLA_HEADS, GLA_DV),
        jnp.pad(gz, ((0, 0), (0, LANE - GLA_GATE_RANK))),
        dk, jnp.tile(ki, (1, IDX_HEADS)), dg, pu, pg], axis=1).astype(bf16)
    wt = jnp.concatenate([
        dq, dv, qi, jnp.pad(wi, ((0, 0), (0, SUBLANE - IDX_HEADS)))], axis=1).T.astype(bf16)
    w2 = jnp.pad(_pad_heads(gate_w2, GLA_HEADS, GLA_DK), ((0, LANE - GLA_GATE_RANK), (0, 0))).astype(bf16)
    b2 = _pad_heads(gate_b[None, :], GLA_HEADS, GLA_DK)
    gn = jnp.pad(gla_norm_g[None, :], ((0, 0), (0, LANE - GLA_DV)))
    wbd = jax.scipy.linalg.block_diag(*[pool_w[g] for g in range(POOL_GROUPS)]).astype(bf16)
    sc = pool_scale[None, :]
    wg = _pad_heads(w_out[:GLA_W].T, GLA_HEADS, GLA_DV).T.astype(bf16)
    wd = w_out[GLA_W:GLA_W + DSA_W].astype(bf16)
    wp = w_out[GLA_W + DSA_W:].astype(bf16)
    return wn, wt, w2, b2, gn, wbd, sc, wg, wd, wp


def _t5_bucket_table(n):
    rel = np.arange(n)
    max_exact = REL_BUCKETS // 2
    relf = np.maximum(rel, 1).astype(np.float32)
    large = max_exact + (np.log(relf / np.float32(max_exact)) / np.float32(math.log(REL_MAX_DIST / max_exact))
                         * np.float32(REL_BUCKETS - max_exact)).astype(np.int32)
    large = np.minimum(large, REL_BUCKETS - 1)
    return np.where(rel < max_exact, rel, large)


def _near_bias(rel_bias, tq):
    bucket = _t5_bucket_table(3 * tq)
    assert np.all(bucket[tq + 1:] == REL_BUCKETS - 1)
    by_rel = (rel_bias[bucket] - rel_bias[REL_BUCKETS - 1][None, :]) * LOG2E
    n = 4 * tq - 1
    u = np.arange(n)
    u = np.where(u < tq, u, u - n)
    g = by_rel[np.maximum(2 * tq + u, 0)].T
    reps = -(-(3 * tq * (n - 1)) // n)
    tiled = jnp.broadcast_to(g[:, None, :], (g.shape[0], reps, n)).reshape(g.shape[0], reps * n)
    tiles = tiled[:, :3 * tq * (n - 1)].reshape(-1, 3 * tq, n - 1)[:, :, :tq]
    bmax = jnp.max(by_rel, axis=0)
    bmax = jnp.broadcast_to(jnp.pad(bmax, (0, SUBLANE - DSA_HEADS))[:, None], (SUBLANE, LANE))
    return tiles, bmax


def kernel(x, norm_g, w_in, gla_gate_w2, gla_gate_b, gla_norm_g, rel_bias, pool_w, pool_scale, w_out,
           final_norm_g):
    B, T, D = x.shape
    assert D == D_MODEL
    tm = min(512, T)
    tq = min(256, T)
    topk = min(TOPK_MAX, T // 4)
    assert T % tm == 0 and T % (2 * tq) == 0 and tq >= LANE and T % GLA_CHUNK == 0
    bias, bmax = _near_bias(rel_bias.astype(f32), tq)
    fg = final_norm_g[None, :]
    for l in range(DEPTH):
        wn, wt, w2, b2, gn, wbd, sc, wg, wd, wp = _layer_weights(
            w_in[l], gla_gate_w2[l], gla_gate_b[l], gla_norm_g[l], pool_w[l], pool_scale[l], w_out[l])
        (gq, gk, gv, gg, gz, dk, kx, dg, pu, pg, dqT, dvT, qiT, wiT) = _in_proj(x, norm_g[l][None, :], wn, wt, tm)
        yg = _gla(gq, gk, gv, gg, gz, w2, b2, gn, tm)
        yp = _pool(pu, pg, wbd, sc, tm)
        yd = _dsa(dqT, qiT, wiT, dk, kx, dvT, dg, bias, bmax, tq, topk)
        x = _out_proj(x, yg, yd, yp, wg, wd, wp, fg, tm, final=(l == DEPTH - 1))
    return x
```

```python
import functools
import math

import numpy as np
import jax
import jax.numpy as jnp
from jax import lax
from jax.experimental import pallas as pl
from jax.experimental.pallas import tpu as pltpu

D_MODEL = 1024
DEPTH = 4
EPS = 1e-6
GLA_HEADS = 4
GLA_DK = 48
GLA_DV = 96
GLA_GATE_RANK = 16
GLA_GATE_NORM = 16.0
GLA_CHUNK = 64
DSA_HEADS = 6
DSA_DH = 64
IDX_HEADS = 4
IDX_DIM = 32
TOPK_MAX = 256
POOL_GROUPS = 4
POOL_GC = 64
POOL_WINDOWS = (2, 4, 8, 16)
REL_BUCKETS = 32
REL_MAX_DIST = 128

GLA_W = GLA_HEADS * GLA_DV
DSA_W = DSA_HEADS * DSA_DH
POOL_W = POOL_GROUPS * POOL_GC
IN_SIZES = (GLA_HEADS * GLA_DK, GLA_HEADS * GLA_DK, GLA_W, GLA_GATE_RANK, GLA_W,
            DSA_W, DSA_W, DSA_W, DSA_W, IDX_HEADS * IDX_DIM, IDX_DIM, IDX_HEADS,
            POOL_W, POOL_W)

LANE = 128
SUBLANE = 8
VMEM_LIMIT_BYTES = 56 * 1024 * 1024

GLA_HP = GLA_HEADS * LANE
POOL_HALO = 16
ONES_ROWS = 16
LOG2E = math.log2(math.e)
DSA_Q_SCALE = (DSA_DH ** -0.5) * LOG2E
SHIFT_MARGIN = 1.01
MIN_DENOM = 2.0 ** -60

_NAT = {}
_off = 0
for _name, _w in (("gq", GLA_HP), ("gk", GLA_HP), ("gv", GLA_HP), ("gg", GLA_HP), ("gz", LANE),
                  ("dk", DSA_W), ("kx", LANE), ("dg", DSA_W), ("pu", POOL_W), ("pg", POOL_W)):
    _NAT[_name] = (_off, _off + _w)
    _off += _w
NAT_COLS = _off
_TR = {}
_off = 0
for _name, _w in (("dq", DSA_W), ("dv", DSA_W), ("qi", IDX_HEADS * IDX_DIM), ("wi", SUBLANE)):
    _TR[_name] = (_off, _off + _w)
    _off += _w
TR_ROWS = _off

INT_MIN = -2 ** 31
NEG_BIG = -1e30

f32 = jnp.float32
bf16 = jnp.bfloat16
i32 = jnp.int32


def _silu(x):
    return x * jax.nn.sigmoid(x)


def _dot(a, b):
    return jnp.dot(a, b, preferred_element_type=f32)


def _dot_nt(a, b):
    return lax.dot_general(a, b, (((1,), (1,)), ((), ())), preferred_element_type=f32)


def _dot_tn(a, b):
    return lax.dot_general(a, b, (((0,), (0,)), ((), ())), preferred_element_type=f32)


def _params():
    return pltpu.CompilerParams(dimension_semantics=("arbitrary", "arbitrary"),
                                vmem_limit_bytes=VMEM_LIMIT_BYTES)


def _in_proj_kernel(x_ref, g_ref, wn_ref, wt_ref,
                    gq_ref, gk_ref, gv_ref, gg_ref, gz_ref, dk_ref, kx_ref, dg_ref, pu_ref, pg_ref,
                    dqT_ref, dvT_ref, qiT_ref, wiT_ref):
    x = x_ref[...]
    h = x * lax.rsqrt(jnp.mean(x * x, axis=-1, keepdims=True) + EPS) * g_ref[...]
    hb = h.astype(bf16)
    for name, ref in (("gq", gq_ref), ("gk", gk_ref), ("gv", gv_ref), ("gg", gg_ref), ("gz", gz_ref),
                      ("dk", dk_ref), ("kx", kx_ref), ("dg", dg_ref), ("pu", pu_ref), ("pg", pg_ref)):
        lo, hi = _NAT[name]
        ref[...] = _dot(hb, wn_ref[:, lo:hi]).astype(ref.dtype)
    for name, ref in (("dq", dqT_ref), ("dv", dvT_ref), ("qi", qiT_ref), ("wi", wiT_ref)):
        lo, hi = _TR[name]
        y = _dot_nt(wt_ref[lo:hi, :], hb)
        if name == "dq":
            y = y * DSA_Q_SCALE
        ref[...] = y.astype(ref.dtype)


def _in_proj(x, g, wn, wt, tm):
    B, T, D = x.shape
    nat_dtypes = {"gq": f32, "gk": f32, "gv": f32, "gg": f32, "gz": f32,
                  "dk": bf16, "kx": bf16, "dg": f32, "pu": f32, "pg": f32}
    tr_dtypes = {"dq": bf16, "dv": bf16, "qi": bf16, "wi": f32}
    out_shape, out_specs = [], []
    for name, dt in nat_dtypes.items():
        w = _NAT[name][1] - _NAT[name][0]
        out_shape.append(jax.ShapeDtypeStruct((B, T, w), dt))
        out_specs.append(pl.BlockSpec((None, tm, w), lambda b, t: (b, t, 0)))
    for name, dt in tr_dtypes.items():
        w = _TR[name][1] - _TR[name][0]
        out_shape.append(jax.ShapeDtypeStruct((B, w, T), dt))
        out_specs.append(pl.BlockSpec((None, w, tm), lambda b, t: (b, 0, t)))
    return pl.pallas_call(
        _in_proj_kernel,
        grid=(B, T // tm),
        in_specs=[pl.BlockSpec((None, tm, D), lambda b, t: (b, t, 0)),
                  pl.BlockSpec((1, D), lambda b, t: (0, 0)),
                  pl.BlockSpec((D, NAT_COLS), lambda b, t: (0, 0)),
                  pl.BlockSpec((TR_ROWS, D), lambda b, t: (0, 0))],
        out_specs=out_specs,
        out_shape=out_shape,
        compiler_params=_params(),
        name="in_proj",
    )(x, g, wn, wt)


def _gla_kernel(gq_ref, gk_ref, gv_ref, gg_ref, gz_ref, w2_ref, b2_ref, gn_ref, out_ref,
                st_ref, glog_ref):
    C = GLA_CHUNK
    n_chunks = gq_ref.shape[0] // C

    @pl.when(pl.program_id(1) == 0)
    def _():
        st_ref[...] = jnp.zeros_like(st_ref)

    z = _dot(gz_ref[...].astype(bf16), w2_ref[...]) + b2_ref[...]
    b = (jnp.minimum(z, 0.0) - jnp.log1p(jnp.exp(-jnp.abs(z)))) / GLA_GATE_NORM
    pos = lax.broadcasted_iota(i32, b.shape, 0) & (C - 1)
    shift = 1
    while shift < C:
        b = b + jnp.where(pos >= shift, pltpu.roll(b, shift, 0), 0.0)
        shift *= 2
    glog_ref[...] = b

    causal = lax.broadcasted_iota(i32, (C, C), 1) <= lax.broadcasted_iota(i32, (C, C), 0)
    gn = gn_ref[...]
    heads = range(GLA_HEADS)
    cols = [slice(h * LANE, (h + 1) * LANE) for h in heads]

    group = 2 if n_chunks % 2 == 0 else 1

    def chunks(c, carry):
        rs = [pl.ds(pl.multiple_of((c * group + t) * C, C), C) for t in range(group)]
        bs = [[glog_ref[r, cs] for cs in cols] for r in rs]
        b_last = [[bb[C - 1:C, :] for bb in bt] for bt in bs]
        ks = [[gk_ref[r, cs] for cs in cols] for r in rs]
        vs = [[gv_ref[r, cs].astype(bf16) for cs in cols] for r in rs]
        qe = [[(gq_ref[rs[t], cols[h]] * jnp.exp(bs[t][h]) * (GLA_DK ** -0.5)).astype(bf16) for h in heads]
              for t in range(group)]
        ke = [[(ks[t][h] * jnp.exp(-bs[t][h])).astype(bf16) for h in heads] for t in range(group)]
        kd = [[(ks[t][h] * jnp.exp(b_last[t][h] - bs[t][h])).astype(bf16) for h in heads] for t in range(group)]
        a = [[_dot_nt(qe[t][h], ke[t][h]) for h in heads] for t in range(group)]
        u = [[_dot_tn(vs[t][h], kd[t][h]) for h in heads] for t in range(group)]
        st = [st_ref[h] for h in heads]
        inter = []
        for t in range(group):
            inter.append([_dot_nt(qe[t][h], st[h].astype(bf16)) for h in heads])
            st = [st[h] * jnp.exp(b_last[t][h]) + u[t][h] for h in heads]
        for h in heads:
            st_ref[h] = st[h]
        for t in range(group):
            for h in heads:
                o = _dot(jnp.where(causal, a[t][h], 0.0).astype(bf16), vs[t][h]) + inter[t][h]
                ms = jnp.sum(o * o, axis=-1, keepdims=True) * (1.0 / GLA_DV)
                y = o * lax.rsqrt(ms + EPS) * gn * _silu(gg_ref[rs[t], cols[h]])
                out_ref[rs[t], cols[h]] = y.astype(out_ref.dtype)
        return carry

    lax.fori_loop(0, n_chunks // group, chunks, 0)


def _gla(gq, gk, gv, gg, gz, w2, b2, gn, tg):
    B, T, _ = gq.shape
    tok = lambda w: pl.BlockSpec((None, tg, w), lambda b, t: (b, t, 0))
    const = lambda s: pl.BlockSpec(s, lambda b, t: (0, 0))
    return pl.pallas_call(
        _gla_kernel,
        grid=(B, T // tg),
        in_specs=[tok(GLA_HP), tok(GLA_HP), tok(GLA_HP), tok(GLA_HP), tok(LANE),
                  const((LANE, GLA_HP)), const((1, GLA_HP)), const((1, LANE))],
        out_specs=tok(GLA_HP),
        out_shape=jax.ShapeDtypeStruct((B, T, GLA_HP), bf16),
        scratch_shapes=[pltpu.VMEM((GLA_HEADS, LANE, LANE), f32),
                        pltpu.VMEM((tg, GLA_HP), f32)],
        compiler_params=_params(),
        name="gla",
    )(gq, gk, gv, gg, gz, w2, b2, gn)


def _pool_kernel(pu_ref, pg_ref, w_ref, sc_ref, out_ref, halo_ref):
    tp = pu_ref.shape[0]
    H = POOL_HALO

    @pl.when(pl.program_id(1) == 0)
    def _():
        halo_ref[...] = jnp.zeros_like(halo_ref)

    u = pu_ref[...]
    ext = jnp.concatenate([halo_ref[...], u], axis=0)
    halo_ref[...] = u[tp - H:, :]
    s2 = ext + pltpu.roll(ext, 1, 0)
    s4 = s2 + pltpu.roll(s2, 2, 0)
    s8 = s4 + pltpu.roll(s4, 4, 0)
    s16 = s8 + pltpu.roll(s8, 8, 0)
    lane = lax.broadcasted_iota(i32, (tp, POOL_W), 1)
    grp = lane // POOL_GC
    sw = jnp.where(grp == 0, s2[H:], jnp.where(grp == 1, s4[H:], jnp.where(grp == 2, s8[H:], s16[H:])))
    win = jnp.where(grp == 0, POOL_WINDOWS[0],
                    jnp.where(grp == 1, POOL_WINDOWS[1],
                              jnp.where(grp == 2, POOL_WINDOWS[2], POOL_WINDOWS[3])))
    t = pl.program_id(1) * tp + lax.broadcasted_iota(i32, (tp, POOL_W), 0)
    cnt = jnp.minimum(t + 1, win).astype(f32)
    pooled = sw / cnt - u
    y = _dot(pooled.astype(bf16), w_ref[...]) * sc_ref[...]
    out_ref[...] = (y * _silu(pg_ref[...])).astype(out_ref.dtype)


def _pool(pu, pg, wbd, sc, tp):
    B, T, _ = pu.shape
    tok = pl.BlockSpec((None, tp, POOL_W), lambda b, t: (b, t, 0))
    return pl.pallas_call(
        _pool_kernel,
        grid=(B, T // tp),
        in_specs=[tok, tok,
                  pl.BlockSpec((POOL_W, POOL_W), lambda b, t: (0, 0)),
                  pl.BlockSpec((1, POOL_W), lambda b, t: (0, 0))],
        out_specs=tok,
        out_shape=jax.ShapeDtypeStruct((B, T, POOL_W), bf16),
        scratch_shapes=[pltpu.VMEM((POOL_HALO, POOL_W), f32)],
        compiler_params=_params(),
        name="pool",
    )(pu, pg, wbd, sc)


def _dsa_kernel(qT_ref, qiT_ref, wiT_ref, k_ref, kx_ref, vT_ref, dg_ref, brow_ref, bmax_ref, out_ref,
                keys_ref, planes_ref, active_ref, qm_ref, qim_ref, acc_ref, kmax_ref, bias_ref, *, topk):
    tq = qT_ref.shape[1]
    tk = tq
    qb = pl.program_id(1)
    n_blocks = qb + 1

    def rows(kb):
        return pl.ds(pl.multiple_of(kb * tk, tk), tk)

    @pl.when(qb == 0)
    def _():
        def body(kb, mx):
            kk = k_ref[rows(kb), :].astype(f32)
            n2 = jnp.sum(kk * kk, axis=1, keepdims=True)
            return jnp.maximum(mx, jnp.max(n2, axis=0, keepdims=True))
        mx = lax.fori_loop(0, k_ref.shape[0] // tk, body, jnp.zeros((1, 1), f32))
        kmax_ref[...] = jnp.broadcast_to(jnp.sqrt(mx), kmax_ref.shape)

    prow = lax.broadcasted_iota(i32, (LANE, tq), 0)
    for h in range(DSA_HEADS):
        pair = qT_ref[(h // 2) * LANE:(h // 2 + 1) * LANE, :]
        qm_ref[h // 2, :, (h % 2) * tq:(h % 2 + 1) * tq] = jnp.where(
            (prow // DSA_DH) == (h % 2), pair, jnp.zeros_like(pair))
    qi = qiT_ref[...]
    for h in range(IDX_HEADS):
        qim_ref[:, h * tq:(h + 1) * tq] = jnp.where((prow // IDX_DIM) == h, qi, jnp.zeros_like(qi))
    w = wiT_ref[...] * ((IDX_DIM ** -0.5) * (IDX_HEADS ** -0.5))

    def score_keys(r):
        d = _dot(kx_ref[r, :], qim_ref[...])
        s = jnp.zeros((d.shape[0], tq), f32)
        for h in range(IDX_HEADS):
            s = s + jnp.maximum(d[:, h * tq:(h + 1) * tq], 0.0) * w[h:h + 1, :]
        bits = lax.bitcast_convert_type(s, i32)
        return jnp.where(bits < 0, INT_MIN - bits, bits)

    def store_keys(kb, keys):
        keys_ref[rows(kb), :] = keys
        u = keys ^ INT_MIN
        words = [u[SUBLANE * k:SUBLANE * (k + 1), :] for k in range(32)]
        j, m = 16, 0x0000FFFF
        while j:
            for k in range(32):
                if k & j == 0:
                    t = (lax.shift_right_logical(words[k], jnp.int32(j)) ^ words[k + j]) & m
                    words[k + j] = words[k + j] ^ t
                    words[k] = words[k] ^ lax.shift_left(t, jnp.int32(j))
            j >>= 1
            m ^= m << j
        r8 = pl.ds(pl.multiple_of(kb * SUBLANE, SUBLANE), SUBLANE)
        for b in range(32):
            planes_ref[b, r8, :] = words[b]

    def score_body(i, c, span):
        keys = score_keys(pl.ds(pl.multiple_of(i * span, span), span))
        for part in range(span // tk):
            store_keys(i * (span // tk) + part, keys[part * tk:(part + 1) * tk, :])
        return c

    @pl.when((pl.program_id(0) == 0) & (qb == 0))
    def _():
        planes_ref[...] = jnp.zeros_like(planes_ref)
        for h in range(DSA_HEADS):
            for c in range(bias_ref.shape[1] // tk):
                x = jnp.broadcast_to(brow_ref[h:h + 1, :], (tk, brow_ref.shape[1]))
                x = pltpu.roll(x, c * tk, 1, stride=1, stride_axis=0)
                bias_ref[h, c * tk:(c + 1) * tk, :] = x[:, :tq]

    lax.fori_loop(0, qb // 2, functools.partial(score_body, span=2 * tk), 0)
    lax.fori_loop(2 * (qb // 2), qb, functools.partial(score_body, span=tk), 0)
    krow = lax.broadcasted_iota(i32, (tk, tq), 0)
    qcol = lax.broadcasted_iota(i32, (tk, tq), 1)
    store_keys(qb, jnp.where(krow <= qcol, score_keys(rows(qb)), INT_MIN))

    @pl.when(n_blocks % 2 == 1)
    def _():
        keys_ref[rows(n_blocks), :] = jnp.full((tk, tq), INT_MIN, i32)

    n_pairs = (n_blocks + 1) // 2

    n_rows = planes_ref.shape[1]
    prow8 = lax.broadcasted_iota(i32, (n_rows, tq), 0)
    active_ref[...] = jnp.where(prow8 < n_blocks * SUBLANE, -1, 0)

    def radix_step(i, carry):
        n_gt, tau_u = carry
        b = 31 - i
        plane = planes_ref[b]
        ones = lax.population_count(active_ref[...] & plane)
        n_set = jnp.sum(jnp.sum(ones.reshape(n_rows // SUBLANE, SUBLANE, tq), axis=0), axis=0, keepdims=True)
        take = n_gt + n_set >= topk
        active_ref[...] = active_ref[...] & (plane ^ jnp.where(take, 0, -1))
        return (jnp.where(take, n_gt, n_gt + n_set),
                tau_u | jnp.where(take, lax.shift_left(jnp.int32(1), b), 0))

    n_gt, tau_u = lax.fori_loop(0, 32, radix_step, (jnp.zeros((1, tq), i32), jnp.zeros((1, tq), i32)))
    tau = tau_u ^ INT_MIN
    need = jnp.where(tau == INT_MIN, 0, topk - n_gt).astype(f32)

    tri = (lax.broadcasted_iota(i32, (tk, tk), 1) <= lax.broadcasted_iota(i32, (tk, tk), 0)).astype(bf16)

    def mask_body(p, run):
        ks = [keys_ref[rows(2 * p + half), :] for half in range(2)]
        eqs = [k == tau for k in ks]
        cnt = [_dot(tri, e.astype(bf16)) for e in eqs]
        for half in range(2):
            pre = cnt[half] + run
            sel = (ks[half] > tau) | (eqs[half] & (pre <= need))
            keys_ref[rows(2 * p + half), :] = lax.bitcast_convert_type(
                jnp.where(sel, 0.0, NEG_BIG).astype(f32), i32)
            run = pre[tk - 1:tk, :]
        return run

    lax.fori_loop(0, n_pairs, mask_body, jnp.zeros((1, tq), f32))

    def tile_ops(i, span, near):
        r = pl.ds(pl.multiple_of(i * span, span), span)
        ones = jnp.ones((ONES_ROWS, span), bf16)
        madd = lax.bitcast_convert_type(keys_ref[r, :], f32)
        dots = [_dot(k_ref[r, g * LANE:(g + 1) * LANE], qm_ref[g]) for g in range(DSA_HEADS // 2)]

        def logits(h):
            lg = dots[h // 2][:, (h % 2) * tq:(h % 2 + 1) * tq] + madd
            if near:
                j = i - (qb - 2)
                lg = lg + bias_ref[h, pl.ds(pl.multiple_of(j * tk, tk), tk), :]
            return lg

        def v1(h):
            return jnp.concatenate([vT_ref[h * DSA_DH:(h + 1) * DSA_DH, r], ones], axis=0)

        return logits, v1

    def attend_online(i, ms, span, near):
        logits, v1 = tile_ops(i, span, near)
        new_ms = []
        for h in range(DSA_HEADS):
            lg = logits(h)
            m_new = jnp.maximum(ms[h], jnp.max(lg, axis=0, keepdims=True))
            alpha = jnp.exp2(ms[h] - m_new)
            acc_ref[h] = alpha * acc_ref[h] + _dot(v1(h), jnp.exp2(lg - m_new).astype(bf16))
            new_ms.append(m_new)
        return tuple(new_ms)

    def attend_fixed(i, shift, span, near):
        logits, v1 = tile_ops(i, span, near)
        for h in range(DSA_HEADS):
            acc_ref[h] = acc_ref[h] + _dot(v1(h), jnp.exp2(logits(h) - shift[h]).astype(bf16))
        return shift

    def sweep(body, carry):
        n_far_pairs = jnp.maximum(qb - 1, 0) // 2
        acc_ref[...] = jnp.zeros_like(acc_ref)
        carry = lax.fori_loop(0, n_far_pairs, functools.partial(body, span=2 * tk, near=False), carry)
        return lax.fori_loop(2 * n_far_pairs, n_blocks, functools.partial(body, span=tk, near=True), carry)

    q_norm = []
    for g in range(DSA_HEADS // 2):
        qf = qm_ref[g].astype(f32)
        n2 = jnp.sum(qf * qf, axis=0, keepdims=True)
        q_norm += [jnp.sqrt(n2[:, :tq]), jnp.sqrt(n2[:, tq:])]
    shift = tuple(q_norm[h] * (kmax_ref[0:1, 0:1] * SHIFT_MARGIN) + bmax_ref[h:h + 1, 0:1]
                  for h in range(DSA_HEADS))
    sweep(attend_fixed, shift)
    denom = acc_ref[0, DSA_DH:DSA_DH + 1, :]
    for h in range(1, DSA_HEADS):
        denom = jnp.minimum(denom, acc_ref[h, DSA_DH:DSA_DH + 1, :])
    accurate = jnp.min(denom) >= MIN_DENOM

    @pl.when(jnp.logical_not(accurate))
    def _():
        sweep(attend_online, tuple(jnp.full((1, tq), NEG_BIG, f32) for _ in range(DSA_HEADS)))

    oT = jnp.concatenate([acc_ref[h, :DSA_DH, :] / acc_ref[h, DSA_DH:DSA_DH + 1, :]
                          for h in range(DSA_HEADS)], axis=0)
    out_ref[...] = (oT.T * _silu(dg_ref[...])).astype(out_ref.dtype)


def _dsa(dqT, qiT, wiT, dk, kx, dvT, dg, bias, bmax, tq, topk):
    B, T, _ = dk.shape
    whole = lambda shape, imap: pl.BlockSpec(shape, imap, pipeline_mode=pl.Buffered(1))
    return pl.pallas_call(
        functools.partial(_dsa_kernel, topk=topk),
        grid=(B, T // tq),
        in_specs=[pl.BlockSpec((None, DSA_W, tq), lambda b, q: (b, 0, q)),
                  pl.BlockSpec((None, IDX_HEADS * IDX_DIM, tq), lambda b, q: (b, 0, q)),
                  pl.BlockSpec((None, SUBLANE, tq), lambda b, q: (b, 0, q)),
                  whole((None, T, DSA_W), lambda b, q: (b, 0, 0)),
                  whole((None, T, LANE), lambda b, q: (b, 0, 0)),
                  whole((None, DSA_W, T), lambda b, q: (b, 0, 0)),
                  pl.BlockSpec((None, tq, DSA_W), lambda b, q: (b, q, 0)),
                  pl.BlockSpec((SUBLANE, 4 * tq), lambda b, q: (0, 0)),
                  pl.BlockSpec((SUBLANE, LANE), lambda b, q: (0, 0))],
        out_specs=pl.BlockSpec((None, tq, DSA_W), lambda b, q: (b, q, 0)),
        out_shape=jax.ShapeDtypeStruct((B, T, DSA_W), bf16),
        scratch_shapes=[pltpu.VMEM((T + tq, tq), i32),
                        pltpu.VMEM((32, T // 32, tq), i32),
                        pltpu.VMEM((T // 32, tq), i32),
                        pltpu.VMEM((DSA_HEADS // 2, LANE, 2 * tq), bf16),
                        pltpu.VMEM((LANE, IDX_HEADS * tq), bf16),
                        pltpu.VMEM((DSA_HEADS, DSA_DH + ONES_ROWS, tq), f32),
                        pltpu.VMEM((SUBLANE, LANE), f32),
                        pltpu.VMEM((DSA_HEADS, 3 * tq, tq), f32)],
        compiler_params=_params(),
        name="dsa",
    )(dqT, qiT, wiT, dk, kx, dvT, dg, bias, bmax)


def _out_proj_kernel(x_ref, yg_ref, yd_ref, yp_ref, wg_ref, wd_ref, wp_ref, fg_ref, out_ref, *, final):
    x = x_ref[...] + _dot(yg_ref[...], wg_ref[...]) + _dot(yd_ref[...], wd_ref[...]) \
        + _dot(yp_ref[...], wp_ref[...])
    if final:
        x = x * lax.rsqrt(jnp.mean(x * x, axis=-1, keepdims=True) + EPS) * fg_ref[...]
    out_ref[...] = x


def _out_proj(x, yg, yd, yp, wg, wd, wp, fg, tm, final):
    B, T, D = x.shape
    tok = lambda w: pl.BlockSpec((None, tm, w), lambda b, t: (b, t, 0))
    const = lambda s: pl.BlockSpec(s, lambda b, t: (0, 0))
    return pl.pallas_call(
        functools.partial(_out_proj_kernel, final=final),
        grid=(B, T // tm),
        in_specs=[tok(D), tok(GLA_HP), tok(DSA_W), tok(POOL_W),
                  const((GLA_HP, D)), const((DSA_W, D)), const((POOL_W, D)), const((1, D))],
        out_specs=tok(D),
        out_shape=jax.ShapeDtypeStruct((B, T, D), f32),
        compiler_params=_params(),
        name="out_proj_final" if final else "out_proj",
    )(x, yg, yd, yp, wg, wd, wp, fg)


def _pad_heads(w, heads, d):
    lead = w.shape[:-1]
    w = w.reshape(*lead, heads, d)
    w = jnp.pad(w, [(0, 0)] * len(lead) + [(0, 0), (0, LANE - d)])
    return w.reshape(*lead, heads * LANE)


def _split_cols(w):
    outs, off = [], 0
    for s in IN_SIZES:
        outs.append(w[..., off:off + s])
        off += s
    return outs


def _layer_weights(w_in, gate_w2, gate_b, gla_norm_g, pool_w, pool_scale, w_out):
    (gq, gk, gv, gz, gg, dq, dk, dv, dg, qi, ki, wi, pu, pg) = _split_cols(w_in)
    D = w_in.shape[0]
    wn = jnp.concatenate([
        _pad_heads(gq, GLA_HEADS, GLA_DK), _pad_heads(gk, GLA_HEADS, GLA_DK),
        _pad_heads(gv, GLA_HEADS, GLA_DV), _pad_heads(gg, GLA_HEADS, GLA_DV),
        jnp.pad(gz, ((0, 0), (0, LANE - GLA_GATE_RANK))),
        dk, jnp.tile(ki, (1, IDX_HEADS)), dg, pu, pg], axis=1).astype(bf16)
    wt = jnp.concatenate([
        dq, dv, qi, jnp.pad(wi, ((0, 0), (0, SUBLANE - IDX_HEADS)))], axis=1).T.astype(bf16)
    w2 = jnp.pad(_pad_heads(gate_w2, GLA_HEADS, GLA_DK), ((0, LANE - GLA_GATE_RANK), (0, 0))).astype(bf16)
    b2 = _pad_heads(gate_b[None, :], GLA_HEADS, GLA_DK)
    gn = jnp.pad(gla_norm_g[None, :], ((0, 0), (0, LANE - GLA_DV)))
    wbd = jax.scipy.linalg.block_diag(*[pool_w[g] for g in range(POOL_GROUPS)]).astype(bf16)
    sc = pool_scale[None, :]
    wg = _pad_heads(w_out[:GLA_W].T, GLA_HEADS, GLA_DV).T.astype(bf16)
    wd = w_out[GLA_W:GLA_W + DSA_W].astype(bf16)
    wp = w_out[GLA_W + DSA_W:].astype(bf16)
    return wn, wt, w2, b2, gn, wbd, sc, wg, wd, wp


def _t5_bucket_table(n):
    rel = np.arange(n)
    max_exact = REL_BUCKETS // 2
    relf = np.maximum(rel, 1).astype(np.float32)
    large = max_exact + (np.log(relf / np.float32(max_exact)) / np.float32(math.log(REL_MAX_DIST / max_exact))
                         * np.float32(REL_BUCKETS - max_exact)).astype(np.int32)
    large = np.minimum(large, REL_BUCKETS - 1)
    return np.where(rel < max_exact, rel, large)


def _near_bias(rel_bias, tq):
    bucket = _t5_bucket_table(3 * tq)
    assert np.all(bucket[tq + 1:] == REL_BUCKETS - 1)
    by_rel = (rel_bias[bucket] - rel_bias[REL_BUCKETS - 1][None, :]) * LOG2E
    n = 4 * tq
    u = np.arange(n)
    u = np.where(u < 2 * tq, u, u - n)
    brow = jnp.pad(by_rel[np.maximum(2 * tq + u, 0)].T, ((0, SUBLANE - DSA_HEADS), (0, 0)))
    bmax = jnp.max(by_rel, axis=0)
    bmax = jnp.broadcast_to(jnp.pad(bmax, (0, SUBLANE - DSA_HEADS))[:, None], (SUBLANE, LANE))
    return brow, bmax


def kernel(x, norm_g, w_in, gla_gate_w2, gla_gate_b, gla_norm_g, rel_bias, pool_w, pool_scale, w_out,
           final_norm_g):
    B, T, D = x.shape
    assert D == D_MODEL
    tm = min(512, T)
    tq = min(256, T)
    topk = min(TOPK_MAX, T // 4)
    assert T % tm == 0 and T % (2 * tq) == 0 and tq >= LANE and T % GLA_CHUNK == 0
    bias, bmax = _near_bias(rel_bias.astype(f32), tq)
    fg = final_norm_g[None, :]
    for l in range(DEPTH):
        wn, wt, w2, b2, gn, wbd, sc, wg, wd, wp = _layer_weights(
            w_in[l], gla_gate_w2[l], gla_gate_b[l], gla_norm_g[l], pool_w[l], pool_scale[l], w_out[l])
        (gq, gk, gv, gg, gz, dk, kx, dg, pu, pg, dqT, dvT, qiT, wiT) = _in_proj(x, norm_g[l][None, :], wn, wt, tm)
        yg = _gla(gq, gk, gv, gg, gz, w2, b2, gn, tm)
        yp = _pool(pu, pg, wbd, sc, tm)
        yd = _dsa(dqT, qiT, wiT, dk, kx, dvT, dg, bias, bmax, tq, topk)
        x = _out_proj(x, yg, yd, yp, wg, wd, wp, fg, tm, final=(l == DEPTH - 1))
    return x
```

```python
import functools
import math

import numpy as np
import jax
import jax.numpy as jnp
from jax import lax
from jax.experimental import pallas as pl
from jax.experimental.pallas import tpu as pltpu

D_MODEL = 1024
DEPTH = 4
EPS = 1e-6
GLA_HEADS = 4
GLA_DK = 48
GLA_DV = 96
GLA_GATE_RANK = 16
GLA_GATE_NORM = 16.0
GLA_CHUNK = 64
DSA_HEADS = 6
DSA_DH = 64
IDX_HEADS = 4
IDX_DIM = 32
TOPK_MAX = 256
POOL_GROUPS = 4
POOL_GC = 64
POOL_WINDOWS = (2, 4, 8, 16)
REL_BUCKETS = 32
REL_MAX_DIST = 128

GLA_W = GLA_HEADS * GLA_DV
DSA_W = DSA_HEADS * DSA_DH
POOL_W = POOL_GROUPS * POOL_GC
IN_SIZES = (GLA_HEADS * GLA_DK, GLA_HEADS * GLA_DK, GLA_W, GLA_GATE_RANK, GLA_W,
            DSA_W, DSA_W, DSA_W, DSA_W, IDX_HEADS * IDX_DIM, IDX_DIM, IDX_HEADS,
            POOL_W, POOL_W)

LANE = 128
SUBLANE = 8
VMEM_LIMIT_BYTES = 56 * 1024 * 1024

GLA_HP = GLA_HEADS * LANE
POOL_HALO = 16
ONES_ROWS = 16
LOG2E = math.log2(math.e)
DSA_Q_SCALE = (DSA_DH ** -0.5) * LOG2E
SHIFT_MARGIN = 1.01
MIN_DENOM = 2.0 ** -60
MASK_GROUP = 4

_NAT = {}
_off = 0
for _name, _w in (("gq", GLA_HP), ("gk", GLA_HP), ("gv", GLA_HP), ("gg", GLA_HP), ("gz", LANE),
                  ("dk", DSA_W), ("kx", LANE), ("dg", DSA_W), ("pu", POOL_W), ("pg", POOL_W)):
    _NAT[_name] = (_off, _off + _w)
    _off += _w
NAT_COLS = _off
_TR = {}
_off = 0
for _name, _w in (("dq", DSA_W), ("dv", DSA_W), ("qi", IDX_HEADS * IDX_DIM), ("wi", SUBLANE)):
    _TR[_name] = (_off, _off + _w)
    _off += _w
TR_ROWS = _off

INT_MIN = -2 ** 31
NEG_BIG = -1e30

f32 = jnp.float32
bf16 = jnp.bfloat16
i32 = jnp.int32


def _silu(x):
    return x * jax.nn.sigmoid(x)


def _dot(a, b):
    return jnp.dot(a, b, preferred_element_type=f32)


def _dot_nt(a, b):
    return lax.dot_general(a, b, (((1,), (1,)), ((), ())), preferred_element_type=f32)


def _dot_tn(a, b):
    return lax.dot_general(a, b, (((0,), (0,)), ((), ())), preferred_element_type=f32)


def _params():
    return pltpu.CompilerParams(dimension_semantics=("arbitrary", "arbitrary"),
                                vmem_limit_bytes=VMEM_LIMIT_BYTES)


def _in_proj_kernel(x_ref, g_ref, wn_ref, wt_ref,
                    gq_ref, gk_ref, gv_ref, gg_ref, gz_ref, dk_ref, kx_ref, dg_ref, pu_ref, pg_ref,
                    dqT_ref, dvT_ref, qiT_ref, wiT_ref):
    x = x_ref[...]
    h = x * lax.rsqrt(jnp.mean(x * x, axis=-1, keepdims=True) + EPS) * g_ref[...]
    hb = h.astype(bf16)
    for name, ref in (("gq", gq_ref), ("gk", gk_ref), ("gv", gv_ref), ("gg", gg_ref), ("gz", gz_ref),
                      ("dk", dk_ref), ("kx", kx_ref), ("dg", dg_ref), ("pu", pu_ref), ("pg", pg_ref)):
        lo, hi = _NAT[name]
        ref[...] = _dot(hb, wn_ref[:, lo:hi]).astype(ref.dtype)
    for name, ref in (("dq", dqT_ref), ("dv", dvT_ref), ("qi", qiT_ref), ("wi", wiT_ref)):
        lo, hi = _TR[name]
        y = _dot_nt(wt_ref[lo:hi, :], hb)
        if name == "dq":
            y = y * DSA_Q_SCALE
        ref[...] = y.astype(ref.dtype)


def _in_proj(x, g, wn, wt, tm):
    B, T, D = x.shape
    nat_dtypes = {"gq": f32, "gk": f32, "gv": f32, "gg": f32, "gz": f32,
                  "dk": bf16, "kx": bf16, "dg": f32, "pu": f32, "pg": f32}
    tr_dtypes = {"dq": bf16, "dv": bf16, "qi": bf16, "wi": f32}
    out_shape, out_specs = [], []
    for name, dt in nat_dtypes.items():
        w = _NAT[name][1] - _NAT[name][0]
        out_shape.append(jax.ShapeDtypeStruct((B, T, w), dt))
        out_specs.append(pl.BlockSpec((None, tm, w), lambda b, t: (b, t, 0)))
    for name, dt in tr_dtypes.items():
        w = _TR[name][1] - _TR[name][0]
        out_shape.append(jax.ShapeDtypeStruct((B, w, T), dt))
        out_specs.append(pl.BlockSpec((None, w, tm), lambda b, t: (b, 0, t)))
    return pl.pallas_call(
        _in_proj_kernel,
        grid=(B, T // tm),
        in_specs=[pl.BlockSpec((None, tm, D), lambda b, t: (b, t, 0)),
                  pl.BlockSpec((1, D), lambda b, t: (0, 0)),
                  pl.BlockSpec((D, NAT_COLS), lambda b, t: (0, 0)),
                  pl.BlockSpec((TR_ROWS, D), lambda b, t: (0, 0))],
        out_specs=out_specs,
        out_shape=out_shape,
        compiler_params=_params(),
        name="in_proj",
    )(x, g, wn, wt)


def _gla_kernel(gq_ref, gk_ref, gv_ref, gg_ref, gz_ref, w2_ref, b2_ref, gn_ref, out_ref,
                st_ref, glog_ref):
    C = GLA_CHUNK
    n_chunks = gq_ref.shape[0] // C

    @pl.when(pl.program_id(1) == 0)
    def _():
        st_ref[...] = jnp.zeros_like(st_ref)

    z = _dot(gz_ref[...].astype(bf16), w2_ref[...]) + b2_ref[...]
    b = (jnp.minimum(z, 0.0) - jnp.log1p(jnp.exp(-jnp.abs(z)))) / GLA_GATE_NORM
    pos = lax.broadcasted_iota(i32, b.shape, 0) & (C - 1)
    shift = 1
    while shift < C:
        b = b + jnp.where(pos >= shift, pltpu.roll(b, shift, 0), 0.0)
        shift *= 2
    glog_ref[...] = b

    causal = lax.broadcasted_iota(i32, (C, C), 1) <= lax.broadcasted_iota(i32, (C, C), 0)
    gn = gn_ref[...]
    heads = range(GLA_HEADS)
    cols = [slice(h * LANE, (h + 1) * LANE) for h in heads]

    group = 2 if n_chunks % 2 == 0 else 1

    def chunks(c, carry):
        rs = [pl.ds(pl.multiple_of((c * group + t) * C, C), C) for t in range(group)]
        bs = [[glog_ref[r, cs] for cs in cols] for r in rs]
        b_last = [[bb[C - 1:C, :] for bb in bt] for bt in bs]
        ks = [[gk_ref[r, cs] for cs in cols] for r in rs]
        vs = [[gv_ref[r, cs].astype(bf16) for cs in cols] for r in rs]
        qe = [[(gq_ref[rs[t], cols[h]] * jnp.exp(bs[t][h]) * (GLA_DK ** -0.5)).astype(bf16) for h in heads]
              for t in range(group)]
        ke = [[(ks[t][h] * jnp.exp(-bs[t][h])).astype(bf16) for h in heads] for t in range(group)]
        kd = [[(ks[t][h] * jnp.exp(b_last[t][h] - bs[t][h])).astype(bf16) for h in heads] for t in range(group)]
        a = [[_dot_nt(qe[t][h], ke[t][h]) for h in heads] for t in range(group)]
        u = [[_dot_tn(vs[t][h], kd[t][h]) for h in heads] for t in range(group)]
        st = [st_ref[h] for h in heads]
        inter = []
        for t in range(group):
            inter.append([_dot_nt(qe[t][h], st[h].astype(bf16)) for h in heads])
            st = [st[h] * jnp.exp(b_last[t][h]) + u[t][h] for h in heads]
        for h in heads:
            st_ref[h] = st[h]
        for t in range(group):
            for h in heads:
                o = _dot(jnp.where(causal, a[t][h], 0.0).astype(bf16), vs[t][h]) + inter[t][h]
                ms = jnp.sum(o * o, axis=-1, keepdims=True) * (1.0 / GLA_DV)
                y = o * lax.rsqrt(ms + EPS) * gn * _silu(gg_ref[rs[t], cols[h]])
                out_ref[rs[t], cols[h]] = y.astype(out_ref.dtype)
        return carry

    lax.fori_loop(0, n_chunks // group, chunks, 0)


def _gla(gq, gk, gv, gg, gz, w2, b2, gn, tg):
    B, T, _ = gq.shape
    tok = lambda w: pl.BlockSpec((None, tg, w), lambda b, t: (b, t, 0))
    const = lambda s: pl.BlockSpec(s, lambda b, t: (0, 0))
    return pl.pallas_call(
        _gla_kernel,
        grid=(B, T // tg),
        in_specs=[tok(GLA_HP), tok(GLA_HP), tok(GLA_HP), tok(GLA_HP), tok(LANE),
                  const((LANE, GLA_HP)), const((1, GLA_HP)), const((1, LANE))],
        out_specs=tok(GLA_HP),
        out_shape=jax.ShapeDtypeStruct((B, T, GLA_HP), bf16),
        scratch_shapes=[pltpu.VMEM((GLA_HEADS, LANE, LANE), f32),
                        pltpu.VMEM((tg, GLA_HP), f32)],
        compiler_params=_params(),
        name="gla",
    )(gq, gk, gv, gg, gz, w2, b2, gn)


def _pool_kernel(pu_ref, pg_ref, w_ref, sc_ref, out_ref, halo_ref):
    tp = pu_ref.shape[0]
    H = POOL_HALO

    @pl.when(pl.program_id(1) == 0)
    def _():
        halo_ref[...] = jnp.zeros_like(halo_ref)

    u = pu_ref[...]
    ext = jnp.concatenate([halo_ref[...], u], axis=0)
    halo_ref[...] = u[tp - H:, :]
    s2 = ext + pltpu.roll(ext, 1, 0)
    s4 = s2 + pltpu.roll(s2, 2, 0)
    s8 = s4 + pltpu.roll(s4, 4, 0)
    s16 = s8 + pltpu.roll(s8, 8, 0)
    lane = lax.broadcasted_iota(i32, (tp, POOL_W), 1)
    grp = lane // POOL_GC
    sw = jnp.where(grp == 0, s2[H:], jnp.where(grp == 1, s4[H:], jnp.where(grp == 2, s8[H:], s16[H:])))
    win = jnp.where(grp == 0, POOL_WINDOWS[0],
                    jnp.where(grp == 1, POOL_WINDOWS[1],
                              jnp.where(grp == 2, POOL_WINDOWS[2], POOL_WINDOWS[3])))
    t = pl.program_id(1) * tp + lax.broadcasted_iota(i32, (tp, POOL_W), 0)
    cnt = jnp.minimum(t + 1, win).astype(f32)
    pooled = sw / cnt - u
    y = _dot(pooled.astype(bf16), w_ref[...]) * sc_ref[...]
    out_ref[...] = (y * _silu(pg_ref[...])).astype(out_ref.dtype)


def _pool(pu, pg, wbd, sc, tp):
    B, T, _ = pu.shape
    tok = pl.BlockSpec((None, tp, POOL_W), lambda b, t: (b, t, 0))
    return pl.pallas_call(
        _pool_kernel,
        grid=(B, T // tp),
        in_specs=[tok, tok,
                  pl.BlockSpec((POOL_W, POOL_W), lambda b, t: (0, 0)),
                  pl.BlockSpec((1, POOL_W), lambda b, t: (0, 0))],
        out_specs=tok,
        out_shape=jax.ShapeDtypeStruct((B, T, POOL_W), bf16),
        scratch_shapes=[pltpu.VMEM((POOL_HALO, POOL_W), f32)],
        compiler_params=_params(),
        name="pool",
    )(pu, pg, wbd, sc)


def _dsa_kernel(qT_ref, qiT_ref, wiT_ref, k_ref, kx_ref, vT_ref, dg_ref, brow_ref, bmax_ref, out_ref,
                keys_ref, planes_ref, active_ref, qm_ref, qim_ref, acc_ref, kmax_ref, bias_ref, *, topk):
    tq = qT_ref.shape[1]
    tk = tq
    qb = pl.program_id(1)
    n_blocks = qb + 1

    def rows(kb):
        return pl.ds(pl.multiple_of(kb * tk, tk), tk)

    @pl.when(qb == 0)
    def _():
        def body(kb, mx):
            kk = k_ref[rows(kb), :].astype(f32)
            n2 = jnp.sum(kk * kk, axis=1, keepdims=True)
            return jnp.maximum(mx, jnp.max(n2, axis=0, keepdims=True))
        mx = lax.fori_loop(0, k_ref.shape[0] // tk, body, jnp.zeros((1, 1), f32))
        kmax_ref[...] = jnp.broadcast_to(jnp.sqrt(mx), kmax_ref.shape)

    prow = lax.broadcasted_iota(i32, (LANE, tq), 0)
    for h in range(DSA_HEADS):
        pair = qT_ref[(h // 2) * LANE:(h // 2 + 1) * LANE, :]
        qm_ref[h // 2, :, (h % 2) * tq:(h % 2 + 1) * tq] = jnp.where(
            (prow // DSA_DH) == (h % 2), pair, jnp.zeros_like(pair))
    qi = qiT_ref[...]
    for h in range(IDX_HEADS):
        qim_ref[:, h * tq:(h + 1) * tq] = jnp.where((prow // IDX_DIM) == h, qi, jnp.zeros_like(qi))
    w = wiT_ref[...] * ((IDX_DIM ** -0.5) * (IDX_HEADS ** -0.5))

    def score_keys(r):
        d = _dot(kx_ref[r, :], qim_ref[...])
        s = jnp.zeros((d.shape[0], tq), f32)
        for h in range(IDX_HEADS):
            s = s + jnp.maximum(d[:, h * tq:(h + 1) * tq], 0.0) * w[h:h + 1, :]
        bits = lax.bitcast_convert_type(s, i32)
        return jnp.where(bits < 0, INT_MIN - bits, bits)

    def store_keys(kb, keys):
        keys_ref[rows(kb), :] = keys
        u = keys ^ INT_MIN
        words = [u[SUBLANE * k:SUBLANE * (k + 1), :] for k in range(32)]
        j, m = 16, 0x0000FFFF
        while j:
            for k in range(32):
                if k & j == 0:
                    t = (lax.shift_right_logical(words[k], jnp.int32(j)) ^ words[k + j]) & m
                    words[k + j] = words[k + j] ^ t
                    words[k] = words[k] ^ lax.shift_left(t, jnp.int32(j))
            j >>= 1
            m ^= m << j
        r8 = pl.ds(pl.multiple_of(kb * SUBLANE, SUBLANE), SUBLANE)
        for b in range(32):
            planes_ref[b, r8, :] = words[b]

    def score_body(i, c, span):
        keys = score_keys(pl.ds(pl.multiple_of(i * span, span), span))
        for part in range(span // tk):
            store_keys(i * (span // tk) + part, keys[part * tk:(part + 1) * tk, :])
        return c

    @pl.when((pl.program_id(0) == 0) & (qb == 0))
    def _():
        planes_ref[...] = jnp.zeros_like(planes_ref)
        for h in range(DSA_HEADS):
            for c in range(bias_ref.shape[1] // tk):
                x = jnp.broadcast_to(brow_ref[h:h + 1, :], (tk, brow_ref.shape[1]))
                x = pltpu.roll(x, c * tk, 1, stride=1, stride_axis=0)
                bias_ref[h, c * tk:(c + 1) * tk, :] = x[:, :tq]

    lax.fori_loop(0, qb // 2, functools.partial(score_body, span=2 * tk), 0)
    lax.fori_loop(2 * (qb // 2), qb, functools.partial(score_body, span=tk), 0)
    krow = lax.broadcasted_iota(i32, (tk, tq), 0)
    qcol = lax.broadcasted_iota(i32, (tk, tq), 1)
    store_keys(qb, jnp.where(krow <= qcol, score_keys(rows(qb)), INT_MIN))

    for extra in range(MASK_GROUP - 1):
        keys_ref[rows(n_blocks + extra), :] = jnp.full((tk, tq), INT_MIN, i32)

    n_rows = planes_ref.shape[1]
    prow8 = lax.broadcasted_iota(i32, (n_rows, tq), 0)
    active_ref[...] = jnp.where(prow8 < n_blocks * SUBLANE, -1, 0)

    half = n_rows // 2
    upper = n_blocks * SUBLANE > half

    def count_set(b, r0):
        ones = lax.population_count(active_ref[r0:r0 + half, :] & planes_ref[b, r0:r0 + half, :])
        return jnp.sum(jnp.sum(ones.reshape(half // SUBLANE, SUBLANE, tq), axis=0), axis=0, keepdims=True)

    def narrow(b, r0, flip):
        active_ref[r0:r0 + half, :] = active_ref[r0:r0 + half, :] & (planes_ref[b, r0:r0 + half, :] ^ flip)

    def radix_step(i, carry):
        n_gt, tau_u = carry
        b = 31 - i
        n_set = count_set(b, 0) + lax.cond(upper, lambda: count_set(b, half),
                                           lambda: jnp.zeros((1, tq), i32))
        take = n_gt + n_set >= topk
        flip = jnp.where(take, 0, -1)
        narrow(b, 0, flip)

        @pl.when(upper)
        def _():
            narrow(b, half, flip)

        return (jnp.where(take, n_gt, n_gt + n_set),
                tau_u | jnp.where(take, lax.shift_left(jnp.int32(1), b), 0))

    n_gt, tau_u = lax.fori_loop(0, 32, radix_step, (jnp.zeros((1, tq), i32), jnp.zeros((1, tq), i32)))
    tau = tau_u ^ INT_MIN
    need = jnp.where(tau == INT_MIN, 0, topk - n_gt).astype(f32)

    tri = (lax.broadcasted_iota(i32, (tk, tk), 1) <= lax.broadcasted_iota(i32, (tk, tk), 0)).astype(bf16)

    def mask_body(g, run):
        ks = [keys_ref[rows(MASK_GROUP * g + part), :] for part in range(MASK_GROUP)]
        eqs = [k == tau for k in ks]
        cnt = [_dot(tri, e.astype(bf16)) for e in eqs]
        for part in range(MASK_GROUP):
            pre = cnt[part] + run
            sel = (ks[part] > tau) | (eqs[part] & (pre <= need))
            keys_ref[rows(MASK_GROUP * g + part), :] = lax.bitcast_convert_type(
                jnp.where(sel, 0.0, NEG_BIG).astype(f32), i32)
            run = pre[tk - 1:tk, :]
        return run

    lax.fori_loop(0, (n_blocks + MASK_GROUP - 1) // MASK_GROUP, mask_body, jnp.zeros((1, tq), f32))

    def tile_ops(i, span, near):
        r = pl.ds(pl.multiple_of(i * span, span), span)
        ones = jnp.ones((ONES_ROWS, span), bf16)
        madd = lax.bitcast_convert_type(keys_ref[r, :], f32)
        dots = [_dot(k_ref[r, g * LANE:(g + 1) * LANE], qm_ref[g]) for g in range(DSA_HEADS // 2)]

        def logits(h):
            lg = dots[h // 2][:, (h % 2) * tq:(h % 2 + 1) * tq] + madd
            if near:
                j = i - (qb - 2)
                lg = lg + bias_ref[h, pl.ds(pl.multiple_of(j * tk, tk), tk), :]
            return lg

        def v1(h):
            return jnp.concatenate([vT_ref[h * DSA_DH:(h + 1) * DSA_DH, r], ones], axis=0)

        return logits, v1

    def attend_online(i, ms, span, near):
        logits, v1 = tile_ops(i, span, near)
        new_ms = []
        for h in range(DSA_HEADS):
            lg = logits(h)
            m_new = jnp.maximum(ms[h], jnp.max(lg, axis=0, keepdims=True))
            alpha = jnp.exp2(ms[h] - m_new)
            acc_ref[h] = alpha * acc_ref[h] + _dot(v1(h), jnp.exp2(lg - m_new).astype(bf16))
            new_ms.append(m_new)
        return tuple(new_ms)

    def attend_fixed(i, shift, span, near):
        logits, v1 = tile_ops(i, span, near)
        for h in range(DSA_HEADS):
            acc_ref[h] = acc_ref[h] + _dot(v1(h), jnp.exp2(logits(h) - shift[h]).astype(bf16))
        return shift

    def sweep(body, carry):
        n_far_pairs = jnp.maximum(qb - 1, 0) // 2
        n_far_quads = n_far_pairs // 2
        acc_ref[...] = jnp.zeros_like(acc_ref)
        carry = lax.fori_loop(0, n_far_quads, functools.partial(body, span=4 * tk, near=False), carry)
        carry = lax.fori_loop(2 * n_far_quads, n_far_pairs, functools.partial(body, span=2 * tk, near=False), carry)
        return lax.fori_loop(2 * n_far_pairs, n_blocks, functools.partial(body, span=tk, near=True), carry)

    q_norm = []
    for g in range(DSA_HEADS // 2):
        qf = qm_ref[g].astype(f32)
        n2 = jnp.sum(qf * qf, axis=0, keepdims=True)
        q_norm += [jnp.sqrt(n2[:, :tq]), jnp.sqrt(n2[:, tq:])]
    shift = tuple(q_norm[h] * (kmax_ref[0:1, 0:1] * SHIFT_MARGIN) + bmax_ref[h:h + 1, 0:1]
                  for h in range(DSA_HEADS))
    sweep(attend_fixed, shift)
    denom = acc_ref[0, DSA_DH:DSA_DH + 1, :]
    for h in range(1, DSA_HEADS):
        denom = jnp.minimum(denom, acc_ref[h, DSA_DH:DSA_DH + 1, :])
    accurate = jnp.min(denom) >= MIN_DENOM

    @pl.when(jnp.logical_not(accurate))
    def _():
        sweep(attend_online, tuple(jnp.full((1, tq), NEG_BIG, f32) for _ in range(DSA_HEADS)))

    oT = jnp.concatenate([acc_ref[h, :DSA_DH, :] / acc_ref[h, DSA_DH:DSA_DH + 1, :]
                          for h in range(DSA_HEADS)], axis=0)
    out_ref[...] = (oT.T * _silu(dg_ref[...])).astype(out_ref.dtype)


def _dsa(dqT, qiT, wiT, dk, kx, dvT, dg, bias, bmax, tq, topk):
    B, T, _ = dk.shape
    whole = lambda shape, imap: pl.BlockSpec(shape, imap, pipeline_mode=pl.Buffered(1))
    return pl.pallas_call(
        functools.partial(_dsa_kernel, topk=topk),
        grid=(B, T // tq),
        in_specs=[pl.BlockSpec((None, DSA_W, tq), lambda b, q: (b, 0, q)),
                  pl.BlockSpec((None, IDX_HEADS * IDX_DIM, tq), lambda b, q: (b, 0, q)),
                  pl.BlockSpec((None, SUBLANE, tq), lambda b, q: (b, 0, q)),
                  whole((None, T, DSA_W), lambda b, q: (b, 0, 0)),
                  whole((None, T, LANE), lambda b, q: (b, 0, 0)),
                  whole((None, DSA_W, T), lambda b, q: (b, 0, 0)),
                  pl.BlockSpec((None, tq, DSA_W), lambda b, q: (b, q, 0)),
                  pl.BlockSpec((SUBLANE, 4 * tq), lambda b, q: (0, 0)),
                  pl.BlockSpec((SUBLANE, LANE), lambda b, q: (0, 0))],
        out_specs=pl.BlockSpec((None, tq, DSA_W), lambda b, q: (b, q, 0)),
        out_shape=jax.ShapeDtypeStruct((B, T, DSA_W), bf16),
        scratch_shapes=[pltpu.VMEM((T + (MASK_GROUP - 1) * tq, tq), i32),
                        pltpu.VMEM((32, T // 32, tq), i32),
                        pltpu.VMEM((T // 32, tq), i32),
                        pltpu.VMEM((DSA_HEADS // 2, LANE, 2 * tq), bf16),
                        pltpu.VMEM((LANE, IDX_HEADS * tq), bf16),
                        pltpu.VMEM((DSA_HEADS, DSA_DH + ONES_ROWS, tq), f32),
                        pltpu.VMEM((SUBLANE, LANE), f32),
                        pltpu.VMEM((DSA_HEADS, 3 * tq, tq), f32)],
        compiler_params=_params(),
        name="dsa",
    )(dqT, qiT, wiT, dk, kx, dvT, dg, bias, bmax)


def _out_proj_kernel(x_ref, yg_ref, yd_ref, yp_ref, wg_ref, wd_ref, wp_ref, fg_ref, out_ref, *, final):
    x = x_ref[...] + _dot(yg_ref[...], wg_ref[...]) + _dot(yd_ref[...], wd_ref[...]) \
        + _dot(yp_ref[...], wp_ref[...])
    if final:
        x = x * lax.rsqrt(jnp.mean(x * x, axis=-1, keepdims=True) + EPS) * fg_ref[...]
    out_ref[...] = x


def _out_proj(x, yg, yd, yp, wg, wd, wp, fg, tm, final):
    B, T, D = x.shape
    tok = lambda w: pl.BlockSpec((None, tm, w), lambda b, t: (b, t, 0))
    const = lambda s: pl.BlockSpec(s, lambda b, t: (0, 0))
    return pl.pallas_call(
        functools.partial(_out_proj_kernel, final=final),
        grid=(B, T // tm),
        in_specs=[tok(D), tok(GLA_HP), tok(DSA_W), tok(POOL_W),
                  const((GLA_HP, D)), const((DSA_W, D)), const((POOL_W, D)), const((1, D))],
        out_specs=tok(D),
        out_shape=jax.ShapeDtypeStruct((B, T, D), f32),
        compiler_params=_params(),
        name="out_proj_final" if final else "out_proj",
    )(x, yg, yd, yp, wg, wd, wp, fg)


def _pad_heads(w, heads, d):
    lead = w.shape[:-1]
    w = w.reshape(*lead, heads, d)
    w = jnp.pad(w, [(0, 0)] * len(lead) + [(0, 0), (0, LANE - d)])
    return w.reshape(*lead, heads * LANE)


def _split_cols(w):
    outs, off = [], 0
    for s in IN_SIZES:
        outs.append(w[..., off:off + s])
        off += s
    return outs


def _layer_weights(w_in, gate_w2, gate_b, gla_norm_g, pool_w, pool_scale, w_out):
    (gq, gk, gv, gz, gg, dq, dk, dv, dg, qi, ki, wi, pu, pg) = _split_cols(w_in)
    D = w_in.shape[0]
    wn = jnp.concatenate([
        _pad_heads(gq, GLA_HEADS, GLA_DK), _pad_heads(gk, GLA_HEADS, GLA_DK),
        _pad_heads(gv, GLA_HEADS, GLA_DV), _pad_heads(gg, GLA_HEADS, GLA_DV),
        jnp.pad(gz, ((0, 0), (0, LANE - GLA_GATE_RANK))),
        dk, jnp.tile(ki, (1, IDX_HEADS)), dg, pu, pg], axis=1).astype(bf16)
    wt = jnp.concatenate([
        dq, dv, qi, jnp.pad(wi, ((0, 0), (0, SUBLANE - IDX_HEADS)))], axis=1).T.astype(bf16)
    w2 = jnp.pad(_pad_heads(gate_w2, GLA_HEADS, GLA_DK), ((0, LANE - GLA_GATE_RANK), (0, 0))).astype(bf16)
    b2 = _pad_heads(gate_b[None, :], GLA_HEADS, GLA_DK)
    gn = jnp.pad(gla_norm_g[None, :], ((0, 0), (0, LANE - GLA_DV)))
    wbd = jax.scipy.linalg.block_diag(*[pool_w[g] for g in range(POOL_GROUPS)]).astype(bf16)
    sc = pool_scale[None, :]
    wg = _pad_heads(w_out[:GLA_W].T, GLA_HEADS, GLA_DV).T.astype(bf16)
    wd = w_out[GLA_W:GLA_W + DSA_W].astype(bf16)
    wp = w_out[GLA_W + DSA_W:].astype(bf16)
    return wn, wt, w2, b2, gn, wbd, sc, wg, wd, wp


def _t5_bucket_table(n):
    rel = np.arange(n)
    max_exact = REL_BUCKETS // 2
    relf = np.maximum(rel, 1).astype(np.float32)
    large = max_exact + (np.log(relf / np.float32(max_exact)) / np.float32(math.log(REL_MAX_DIST / max_exact))
                         * np.float32(REL_BUCKETS - max_exact)).astype(np.int32)
    large = np.minimum(large, REL_BUCKETS - 1)
    return np.where(rel < max_exact, rel, large)


def _near_bias(rel_bias, tq):
    bucket = _t5_bucket_table(3 * tq)
    assert np.all(bucket[tq + 1:] == REL_BUCKETS - 1)
    by_rel = (rel_bias[bucket] - rel_bias[REL_BUCKETS - 1][None, :]) * LOG2E
    n = 4 * tq
    u = np.arange(n)
    u = np.where(u < 2 * tq, u, u - n)
    brow = jnp.pad(by_rel[np.maximum(2 * tq + u, 0)].T, ((0, SUBLANE - DSA_HEADS), (0, 0)))
    bmax = jnp.max(by_rel, axis=0)
    bmax = jnp.broadcast_to(jnp.pad(bmax, (0, SUBLANE - DSA_HEADS))[:, None], (SUBLANE, LANE))
    return brow, bmax


def kernel(x, norm_g, w_in, gla_gate_w2, gla_gate_b, gla_norm_g, rel_bias, pool_w, pool_scale, w_out,
           final_norm_g):
    B, T, D = x.shape
    assert D == D_MODEL
    tm = min(512, T)
    tq = min(256, T)
    topk = min(TOPK_MAX, T // 4)
    assert T % tm == 0 and T % (2 * tq) == 0 and tq >= LANE and T % GLA_CHUNK == 0
    bias, bmax = _near_bias(rel_bias.astype(f32), tq)
    fg = final_norm_g[None, :]
    for l in range(DEPTH):
        wn, wt, w2, b2, gn, wbd, sc, wg, wd, wp = _layer_weights(
            w_in[l], gla_gate_w2[l], gla_gate_b[l], gla_norm_g[l], pool_w[l], pool_scale[l], w_out[l])
        (gq, gk, gv, gg, gz, dk, kx, dg, pu, pg, dqT, dvT, qiT, wiT) = _in_proj(x, norm_g[l][None, :], wn, wt, tm)
        yg = _gla(gq, gk, gv, gg, gz, w2, b2, gn, tm)
        yp = _pool(pu, pg, wbd, sc, tm)
        yd = _dsa(dqT, qiT, wiT, dk, kx, dvT, dg, bias, bmax, tq, topk)
        x = _out_proj(x, yg, yd, yp, wg, wd, wp, fg, tm, final=(l == DEPTH - 1))
    return x
```

```python
import functools
import math

import numpy as np
import jax
import jax.numpy as jnp
from jax import lax
from jax.experimental import pallas as pl
from jax.experimental.pallas import tpu as pltpu

D_MODEL = 1024
DEPTH = 4
EPS = 1e-6
GLA_HEADS = 4
GLA_DK = 48
GLA_DV = 96
GLA_GATE_RANK = 16
GLA_GATE_NORM = 16.0
GLA_CHUNK = 64
DSA_HEADS = 6
DSA_DH = 64
IDX_HEADS = 4
IDX_DIM = 32
TOPK_MAX = 256
POOL_GROUPS = 4
POOL_GC = 64
POOL_WINDOWS = (2, 4, 8, 16)
REL_BUCKETS = 32
REL_MAX_DIST = 128

GLA_W = GLA_HEADS * GLA_DV
DSA_W = DSA_HEADS * DSA_DH
POOL_W = POOL_GROUPS * POOL_GC
IN_SIZES = (GLA_HEADS * GLA_DK, GLA_HEADS * GLA_DK, GLA_W, GLA_GATE_RANK, GLA_W,
            DSA_W, DSA_W, DSA_W, DSA_W, IDX_HEADS * IDX_DIM, IDX_DIM, IDX_HEADS,
            POOL_W, POOL_W)

LANE = 128
SUBLANE = 8
VMEM_LIMIT_BYTES = 56 * 1024 * 1024

GLA_HP = GLA_HEADS * LANE
GLA_QK_SLOT = LANE // 2
GLA_QK_W = GLA_HEADS * GLA_QK_SLOT
assert GLA_DK + GLA_GATE_RANK <= GLA_QK_SLOT
POOL_HALO = 16
ONES_ROWS = 16
LOG2E = math.log2(math.e)
DSA_Q_SCALE = (DSA_DH ** -0.5) * LOG2E
SHIFT_MARGIN = 1.01
MIN_DENOM = 2.0 ** -60
MASK_GROUP = 4

_NAT = {}
_off = 0
for _name, _w in (("gq", GLA_QK_W), ("gk", GLA_QK_W), ("gv", GLA_HP), ("gg", GLA_HP),
                  ("dk", DSA_W), ("kx", LANE), ("dg", DSA_W), ("pu", POOL_W), ("pg", POOL_W)):
    _NAT[_name] = (_off, _off + _w)
    _off += _w
NAT_COLS = _off
_TR = {}
_off = 0
for _name, _w in (("dq", DSA_W), ("dv", DSA_W), ("qi", IDX_HEADS * IDX_DIM), ("wi", SUBLANE)):
    _TR[_name] = (_off, _off + _w)
    _off += _w
TR_ROWS = _off

INT_MIN = -2 ** 31
NEG_BIG = -1e30

f32 = jnp.float32
bf16 = jnp.bfloat16
i32 = jnp.int32


def _silu(x):
    return x * jax.nn.sigmoid(x)


def _dot(a, b):
    return jnp.dot(a, b, preferred_element_type=f32)


def _dot_nt(a, b):
    return lax.dot_general(a, b, (((1,), (1,)), ((), ())), preferred_element_type=f32)


def _dot_tn(a, b):
    return lax.dot_general(a, b, (((0,), (0,)), ((), ())), preferred_element_type=f32)


def _params():
    return pltpu.CompilerParams(dimension_semantics=("arbitrary", "arbitrary"),
                                vmem_limit_bytes=VMEM_LIMIT_BYTES)


def _in_proj_kernel(x_ref, g_ref, wn_ref, wt_ref,
                    gq_ref, gk_ref, gv_ref, gg_ref, dk_ref, kx_ref, dg_ref, pu_ref, pg_ref,
                    dqT_ref, dvT_ref, qiT_ref, wiT_ref):
    x = x_ref[...]
    h = x * lax.rsqrt(jnp.mean(x * x, axis=-1, keepdims=True) + EPS) * g_ref[...]
    hb = h.astype(bf16)
    for name, ref in (("gq", gq_ref), ("gk", gk_ref), ("gv", gv_ref), ("gg", gg_ref),
                      ("dk", dk_ref), ("kx", kx_ref), ("dg", dg_ref), ("pu", pu_ref), ("pg", pg_ref)):
        lo, hi = _NAT[name]
        ref[...] = _dot(hb, wn_ref[:, lo:hi]).astype(ref.dtype)
    for name, ref in (("dq", dqT_ref), ("dv", dvT_ref), ("qi", qiT_ref), ("wi", wiT_ref)):
        lo, hi = _TR[name]
        y = _dot_nt(wt_ref[lo:hi, :], hb)
        if name == "dq":
            y = y * DSA_Q_SCALE
        ref[...] = y.astype(ref.dtype)


def _in_proj(x, g, wn, wt, tm):
    B, T, D = x.shape
    nat_dtypes = {"gq": f32, "gk": f32, "gv": f32, "gg": f32,
                  "dk": bf16, "kx": bf16, "dg": f32, "pu": f32, "pg": f32}
    tr_dtypes = {"dq": bf16, "dv": bf16, "qi": bf16, "wi": f32}
    out_shape, out_specs = [], []
    for name, dt in nat_dtypes.items():
        w = _NAT[name][1] - _NAT[name][0]
        out_shape.append(jax.ShapeDtypeStruct((B, T, w), dt))
        out_specs.append(pl.BlockSpec((None, tm, w), lambda b, t: (b, t, 0)))
    for name, dt in tr_dtypes.items():
        w = _TR[name][1] - _TR[name][0]
        out_shape.append(jax.ShapeDtypeStruct((B, w, T), dt))
        out_specs.append(pl.BlockSpec((None, w, tm), lambda b, t: (b, 0, t)))
    return pl.pallas_call(
        _in_proj_kernel,
        grid=(B, T // tm),
        in_specs=[pl.BlockSpec((None, tm, D), lambda b, t: (b, t, 0)),
                  pl.BlockSpec((1, D), lambda b, t: (0, 0)),
                  pl.BlockSpec((D, NAT_COLS), lambda b, t: (0, 0)),
                  pl.BlockSpec((TR_ROWS, D), lambda b, t: (0, 0))],
        out_specs=out_specs,
        out_shape=out_shape,
        compiler_params=_params(),
        name="in_proj",
    )(x, g, wn, wt)


def _gla_kernel(gq_ref, gk_ref, gv_ref, gg_ref, w2_ref, b2_ref, gn_ref, out_ref,
                st_ref, glog_ref):
    C = GLA_CHUNK
    n_chunks = gq_ref.shape[0] // C

    @pl.when(pl.program_id(1) == 0)
    def _():
        st_ref[...] = jnp.zeros_like(st_ref)

    z = _dot(gq_ref[...].astype(bf16), w2_ref[...]) + b2_ref[...]
    b = (jnp.minimum(z, 0.0) - jnp.log1p(jnp.exp(-jnp.abs(z)))) / GLA_GATE_NORM
    pos = lax.broadcasted_iota(i32, b.shape, 0) & (C - 1)
    shift = 1
    while shift < C:
        b = b + jnp.where(pos >= shift, pltpu.roll(b, shift, 0), 0.0)
        shift *= 2
    glog_ref[...] = b

    causal = lax.broadcasted_iota(i32, (C, C), 1) <= lax.broadcasted_iota(i32, (C, C), 0)
    gn = gn_ref[...]
    heads = range(GLA_HEADS)
    pairs = range(GLA_HEADS // 2)
    cols = [slice(h * LANE, (h + 1) * LANE) for h in heads]
    pair_cols = [slice(g * LANE, (g + 1) * LANE) for g in pairs]
    lane_half = lax.broadcasted_iota(i32, (C, LANE), 1) // GLA_QK_SLOT
    own_half = [lane_half == 0, lane_half == 1]

    group = 2 if n_chunks % 2 == 0 else 1

    def chunks(c, carry):
        rs = [pl.ds(pl.multiple_of((c * group + t) * C, C), C) for t in range(group)]
        bs = [[glog_ref[r, ps] for ps in pair_cols] for r in rs]
        b_last = [[bb[C - 1:C, :] for bb in bt] for bt in bs]
        ks = [[gk_ref[r, ps] for ps in pair_cols] for r in rs]
        vs = [[gv_ref[r, cs].astype(bf16) for cs in cols] for r in rs]
        qe_pair = [[gq_ref[rs[t], pair_cols[g]] * jnp.exp(bs[t][g]) * (GLA_DK ** -0.5) for g in pairs]
                   for t in range(group)]
        qe = [[jnp.where(own_half[h % 2], qe_pair[t][h // 2], 0.0).astype(bf16) for h in heads]
              for t in range(group)]
        ke = [[(ks[t][g] * jnp.exp(-bs[t][g])).astype(bf16) for g in pairs] for t in range(group)]
        kd = [[(ks[t][g] * jnp.exp(b_last[t][g] - bs[t][g])).astype(bf16) for g in pairs] for t in range(group)]
        a = [[_dot_nt(qe[t][h], ke[t][h // 2]) for h in heads] for t in range(group)]
        u = [[_dot_tn(vs[t][h], kd[t][h // 2]) for h in heads] for t in range(group)]
        st = [st_ref[h] for h in heads]
        inter = []
        for t in range(group):
            inter.append([_dot_nt(qe[t][h], st[h].astype(bf16)) for h in heads])
            st = [st[h] * jnp.exp(b_last[t][h // 2]) + u[t][h] for h in heads]
        for h in heads:
            st_ref[h] = st[h]
        for t in range(group):
            for h in heads:
                o = _dot(jnp.where(causal, a[t][h], 0.0).astype(bf16), vs[t][h]) + inter[t][h]
                ms = jnp.sum(o * o, axis=-1, keepdims=True) * (1.0 / GLA_DV)
                y = o * lax.rsqrt(ms + EPS) * gn * _silu(gg_ref[rs[t], cols[h]])
                out_ref[rs[t], cols[h]] = y.astype(out_ref.dtype)
        return carry

    lax.fori_loop(0, n_chunks // group, chunks, 0)


def _gla(gq, gk, gv, gg, w2, b2, gn, tg):
    B, T, _ = gq.shape
    tok = lambda w: pl.BlockSpec((None, tg, w), lambda b, t: (b, t, 0))
    const = lambda s: pl.BlockSpec(s, lambda b, t: (0, 0))
    return pl.pallas_call(
        _gla_kernel,
        grid=(B, T // tg),
        in_specs=[tok(GLA_QK_W), tok(GLA_QK_W), tok(GLA_HP), tok(GLA_HP),
                  const((GLA_QK_W, GLA_QK_W)), const((1, GLA_QK_W)), const((1, LANE))],
        out_specs=tok(GLA_HP),
        out_shape=jax.ShapeDtypeStruct((B, T, GLA_HP), bf16),
        scratch_shapes=[pltpu.VMEM((GLA_HEADS, LANE, LANE), f32),
                        pltpu.VMEM((tg, GLA_QK_W), f32)],
        compiler_params=_params(),
        name="gla",
    )(gq, gk, gv, gg, w2, b2, gn)


def _pool_kernel(pu_ref, pg_ref, w_ref, sc_ref, out_ref, halo_ref):
    tp = pu_ref.shape[0]
    H = POOL_HALO

    @pl.when(pl.program_id(1) == 0)
    def _():
        halo_ref[...] = jnp.zeros_like(halo_ref)

    u = pu_ref[...]
    ext = jnp.concatenate([halo_ref[...], u], axis=0)
    halo_ref[...] = u[tp - H:, :]
    s2 = ext + pltpu.roll(ext, 1, 0)
    s4 = s2 + pltpu.roll(s2, 2, 0)
    s8 = s4 + pltpu.roll(s4, 4, 0)
    s16 = s8 + pltpu.roll(s8, 8, 0)
    lane = lax.broadcasted_iota(i32, (tp, POOL_W), 1)
    grp = lane // POOL_GC
    sw = jnp.where(grp == 0, s2[H:], jnp.where(grp == 1, s4[H:], jnp.where(grp == 2, s8[H:], s16[H:])))
    win = jnp.where(grp == 0, POOL_WINDOWS[0],
                    jnp.where(grp == 1, POOL_WINDOWS[1],
                              jnp.where(grp == 2, POOL_WINDOWS[2], POOL_WINDOWS[3])))
    t = pl.program_id(1) * tp + lax.broadcasted_iota(i32, (tp, POOL_W), 0)
    cnt = jnp.minimum(t + 1, win).astype(f32)
    pooled = sw / cnt - u
    y = _dot(pooled.astype(bf16), w_ref[...]) * sc_ref[...]
    out_ref[...] = (y * _silu(pg_ref[...])).astype(out_ref.dtype)


def _pool(pu, pg, wbd, sc, tp):
    B, T, _ = pu.shape
    tok = pl.BlockSpec((None, tp, POOL_W), lambda b, t: (b, t, 0))
    return pl.pallas_call(
        _pool_kernel,
        grid=(B, T // tp),
        in_specs=[tok, tok,
                  pl.BlockSpec((POOL_W, POOL_W), lambda b, t: (0, 0)),
                  pl.BlockSpec((1, POOL_W), lambda b, t: (0, 0))],
        out_specs=tok,
        out_shape=jax.ShapeDtypeStruct((B, T, POOL_W), bf16),
        scratch_shapes=[pltpu.VMEM((POOL_HALO, POOL_W), f32)],
        compiler_params=_params(),
        name="pool",
    )(pu, pg, wbd, sc)


def _dsa_kernel(qT_ref, qiT_ref, wiT_ref, k_ref, kx_ref, vT_ref, dg_ref, brow_ref, bmax_ref, out_ref,
                keys_ref, planes_ref, active_ref, qm_ref, qim_ref, acc_ref, kmax_ref, bias_ref, *, topk):
    tq = qT_ref.shape[1]
    tk = tq
    qb = pl.program_id(1)
    n_blocks = qb + 1

    def rows(kb):
        return pl.ds(pl.multiple_of(kb * tk, tk), tk)

    @pl.when(qb == 0)
    def _():
        def body(kb, mx):
            kk = k_ref[rows(kb), :].astype(f32)
            n2 = jnp.sum(kk * kk, axis=1, keepdims=True)
            return jnp.maximum(mx, jnp.max(n2, axis=0, keepdims=True))
        mx = lax.fori_loop(0, k_ref.shape[0] // tk, body, jnp.zeros((1, 1), f32))
        kmax_ref[...] = jnp.broadcast_to(jnp.sqrt(mx), kmax_ref.shape)

    prow = lax.broadcasted_iota(i32, (LANE, tq), 0)
    for h in range(DSA_HEADS):
        pair = qT_ref[(h // 2) * LANE:(h // 2 + 1) * LANE, :]
        qm_ref[h // 2, :, (h % 2) * tq:(h % 2 + 1) * tq] = jnp.where(
            (prow // DSA_DH) == (h % 2), pair, jnp.zeros_like(pair))
    qi = qiT_ref[...]
    for h in range(IDX_HEADS):
        qim_ref[:, h * tq:(h + 1) * tq] = jnp.where((prow // IDX_DIM) == h, qi, jnp.zeros_like(qi))
    w = wiT_ref[...] * ((IDX_DIM ** -0.5) * (IDX_HEADS ** -0.5))

    def score_keys(r):
        d = _dot(kx_ref[r, :], qim_ref[...])
        s = jnp.zeros((d.shape[0], tq), f32)
        for h in range(IDX_HEADS):
            s = s + jnp.maximum(d[:, h * tq:(h + 1) * tq], 0.0) * w[h:h + 1, :]
        bits = lax.bitcast_convert_type(s, i32)
        return jnp.where(bits < 0, INT_MIN - bits, bits)

    def store_keys(kb, keys):
        keys_ref[rows(kb), :] = keys
        u = keys ^ INT_MIN
        words = [u[SUBLANE * k:SUBLANE * (k + 1), :] for k in range(32)]
        j, m = 16, 0x0000FFFF
        while j:
            for k in range(32):
                if k & j == 0:
                    t = (lax.shift_right_logical(words[k], jnp.int32(j)) ^ words[k + j]) & m
                    words[k + j] = words[k + j] ^ t
                    words[k] = words[k] ^ lax.shift_left(t, jnp.int32(j))
            j >>= 1
            m ^= m << j
        r8 = pl.ds(pl.multiple_of(kb * SUBLANE, SUBLANE), SUBLANE)
        for b in range(32):
            planes_ref[b, r8, :] = words[b]

    def score_body(i, c, span):
        keys = score_keys(pl.ds(pl.multiple_of(i * span, span), span))
        for part in range(span // tk):
            store_keys(i * (span // tk) + part, keys[part * tk:(part + 1) * tk, :])
        return c

    @pl.when((pl.program_id(0) == 0) & (qb == 0))
    def _():
        planes_ref[...] = jnp.zeros_like(planes_ref)
        for h in range(DSA_HEADS):
            for c in range(bias_ref.shape[1] // tk):
                x = jnp.broadcast_to(brow_ref[h:h + 1, :], (tk, brow_ref.shape[1]))
                x = pltpu.roll(x, c * tk, 1, stride=1, stride_axis=0)
                bias_ref[h, c * tk:(c + 1) * tk, :] = x[:, :tq]

    lax.fori_loop(0, qb // 2, functools.partial(score_body, span=2 * tk), 0)
    lax.fori_loop(2 * (qb // 2), qb, functools.partial(score_body, span=tk), 0)
    krow = lax.broadcasted_iota(i32, (tk, tq), 0)
    qcol = lax.broadcasted_iota(i32, (tk, tq), 1)
    store_keys(qb, jnp.where(krow <= qcol, score_keys(rows(qb)), INT_MIN))

    for extra in range(MASK_GROUP - 1):
        keys_ref[rows(n_blocks + extra), :] = jnp.full((tk, tq), INT_MIN, i32)

    n_rows = planes_ref.shape[1]
    prow8 = lax.broadcasted_iota(i32, (n_rows, tq), 0)
    active_ref[...] = jnp.where(prow8 < n_blocks * SUBLANE, -1, 0)

    half = n_rows // 2
    upper = n_blocks * SUBLANE > half

    def count_set(b, r0):
        ones = lax.population_count(active_ref[r0:r0 + half, :] & planes_ref[b, r0:r0 + half, :])
        return jnp.sum(jnp.sum(ones.reshape(half // SUBLANE, SUBLANE, tq), axis=0), axis=0, keepdims=True)

    def narrow(b, r0, flip):
        active_ref[r0:r0 + half, :] = active_ref[r0:r0 + half, :] & (planes_ref[b, r0:r0 + half, :] ^ flip)

    def radix_step(i, carry):
        n_gt, tau_u = carry
        b = 31 - i
        n_set = count_set(b, 0) + lax.cond(upper, lambda: count_set(b, half),
                                           lambda: jnp.zeros((1, tq), i32))
        take = n_gt + n_set >= topk
        flip = jnp.where(take, 0, -1)
        narrow(b, 0, flip)

        @pl.when(upper)
        def _():
            narrow(b, half, flip)

        return (jnp.where(take, n_gt, n_gt + n_set),
                tau_u | jnp.where(take, lax.shift_left(jnp.int32(1), b), 0))

    n_gt, tau_u = lax.fori_loop(0, 32, radix_step, (jnp.zeros((1, tq), i32), jnp.zeros((1, tq), i32)))
    tau = tau_u ^ INT_MIN
    need = jnp.where(tau == INT_MIN, 0, topk - n_gt).astype(f32)

    tri = (lax.broadcasted_iota(i32, (tk, tk), 1) <= lax.broadcasted_iota(i32, (tk, tk), 0)).astype(bf16)

    def mask_body(g, run):
        ks = [keys_ref[rows(MASK_GROUP * g + part), :] for part in range(MASK_GROUP)]
        eqs = [k == tau for k in ks]
        cnt = [_dot(tri, e.astype(bf16)) for e in eqs]
        for part in range(MASK_GROUP):
            pre = cnt[part] + run
            sel = (ks[part] > tau) | (eqs[part] & (pre <= need))
            keys_ref[rows(MASK_GROUP * g + part), :] = lax.bitcast_convert_type(
                jnp.where(sel, 0.0, NEG_BIG).astype(f32), i32)
            run = pre[tk - 1:tk, :]
        return run

    lax.fori_loop(0, (n_blocks + MASK_GROUP - 1) // MASK_GROUP, mask_body, jnp.zeros((1, tq), f32))

    def tile_ops(i, span, near):
        r = pl.ds(pl.multiple_of(i * span, span), span)
        ones = jnp.ones((ONES_ROWS, span), bf16)
        madd = lax.bitcast_convert_type(keys_ref[r, :], f32)
        dots = [_dot(k_ref[r, g * LANE:(g + 1) * LANE], qm_ref[g]) for g in range(DSA_HEADS // 2)]

        def logits(h):
            lg = dots[h // 2][:, (h % 2) * tq:(h % 2 + 1) * tq] + madd
            if near:
                j = i - (qb - 2)
                lg = lg + bias_ref[h, pl.ds(pl.multiple_of(j * tk, tk), tk), :]
            return lg

        def v1(h):
            return jnp.concatenate([vT_ref[h * DSA_DH:(h + 1) * DSA_DH, r], ones], axis=0)

        return logits, v1

    def attend_online(i, ms, span, near):
        logits, v1 = tile_ops(i, span, near)
        new_ms = []
        for h in range(DSA_HEADS):
            lg = logits(h)
            m_new = jnp.maximum(ms[h], jnp.max(lg, axis=0, keepdims=True))
            alpha = jnp.exp2(ms[h] - m_new)
            acc_ref[h] = alpha * acc_ref[h] + _dot(v1(h), jnp.exp2(lg - m_new).astype(bf16))
            new_ms.append(m_new)
        return tuple(new_ms)

    def attend_fixed(i, shift, span, near):
        logits, v1 = tile_ops(i, span, near)
        for h in range(DSA_HEADS):
            acc_ref[h] = acc_ref[h] + _dot(v1(h), jnp.exp2(logits(h) - shift[h]).astype(bf16))
        return shift

    def sweep(body, carry):
        n_far_pairs = jnp.maximum(qb - 1, 0) // 2
        n_far_quads = n_far_pairs // 2
        acc_ref[...] = jnp.zeros_like(acc_ref)
        carry = lax.fori_loop(0, n_far_quads, functools.partial(body, span=4 * tk, near=False), carry)
        carry = lax.fori_loop(2 * n_far_quads, n_far_pairs, functools.partial(body, span=2 * tk, near=False), carry)
        return lax.fori_loop(2 * n_far_pairs, n_blocks, functools.partial(body, span=tk, near=True), carry)

    q_norm = []
    for g in range(DSA_HEADS // 2):
        qf = qm_ref[g].astype(f32)
        n2 = jnp.sum(qf * qf, axis=0, keepdims=True)
        q_norm += [jnp.sqrt(n2[:, :tq]), jnp.sqrt(n2[:, tq:])]
    shift = tuple(q_norm[h] * (kmax_ref[0:1, 0:1] * SHIFT_MARGIN) + bmax_ref[h:h + 1, 0:1]
                  for h in range(DSA_HEADS))
    sweep(attend_fixed, shift)
    denom = acc_ref[0, DSA_DH:DSA_DH + 1, :]
    for h in range(1, DSA_HEADS):
        denom = jnp.minimum(denom, acc_ref[h, DSA_DH:DSA_DH + 1, :])
    accurate = jnp.min(denom) >= MIN_DENOM

    @pl.when(jnp.logical_not(accurate))
    def _():
        sweep(attend_online, tuple(jnp.full((1, tq), NEG_BIG, f32) for _ in range(DSA_HEADS)))

    oT = jnp.concatenate([acc_ref[h, :DSA_DH, :] / acc_ref[h, DSA_DH:DSA_DH + 1, :]
                          for h in range(DSA_HEADS)], axis=0)
    out_ref[...] = (oT.T * _silu(dg_ref[...])).astype(out_ref.dtype)


def _dsa(dqT, qiT, wiT, dk, kx, dvT, dg, bias, bmax, tq, topk):
    B, T, _ = dk.shape
    whole = lambda shape, imap: pl.BlockSpec(shape, imap, pipeline_mode=pl.Buffered(1))
    return pl.pallas_call(
        functools.partial(_dsa_kernel, topk=topk),
        grid=(B, T // tq),
        in_specs=[pl.BlockSpec((None, DSA_W, tq), lambda b, q: (b, 0, q)),
                  pl.BlockSpec((None, IDX_HEADS * IDX_DIM, tq), lambda b, q: (b, 0, q)),
                  pl.BlockSpec((None, SUBLANE, tq), lambda b, q: (b, 0, q)),
                  whole((None, T, DSA_W), lambda b, q: (b, 0, 0)),
                  whole((None, T, LANE), lambda b, q: (b, 0, 0)),
                  whole((None, DSA_W, T), lambda b, q: (b, 0, 0)),
                  pl.BlockSpec((None, tq, DSA_W), lambda b, q: (b, q, 0)),
                  pl.BlockSpec((SUBLANE, 4 * tq), lambda b, q: (0, 0)),
                  pl.BlockSpec((SUBLANE, LANE), lambda b, q: (0, 0))],
        out_specs=pl.BlockSpec((None, tq, DSA_W), lambda b, q: (b, q, 0)),
        out_shape=jax.ShapeDtypeStruct((B, T, DSA_W), bf16),
        scratch_shapes=[pltpu.VMEM((T + (MASK_GROUP - 1) * tq, tq), i32),
                        pltpu.VMEM((32, T // 32, tq), i32),
                        pltpu.VMEM((T // 32, tq), i32),
                        pltpu.VMEM((DSA_HEADS // 2, LANE, 2 * tq), bf16),
                        pltpu.VMEM((LANE, IDX_HEADS * tq), bf16),
                        pltpu.VMEM((DSA_HEADS, DSA_DH + ONES_ROWS, tq), f32),
                        pltpu.VMEM((SUBLANE, LANE), f32),
                        pltpu.VMEM((DSA_HEADS, 3 * tq, tq), f32)],
        compiler_params=_params(),
        name="dsa",
    )(dqT, qiT, wiT, dk, kx, dvT, dg, bias, bmax)


def _out_proj_kernel(x_ref, yg_ref, yd_ref, yp_ref, wg_ref, wd_ref, wp_ref, fg_ref, out_ref, *, final):
    x = x_ref[...] + _dot(yg_ref[...], wg_ref[...]) + _dot(yd_ref[...], wd_ref[...]) \
        + _dot(yp_ref[...], wp_ref[...])
    if final:
        x = x * lax.rsqrt(jnp.mean(x * x, axis=-1, keepdims=True) + EPS) * fg_ref[...]
    out_ref[...] = x


def _out_proj(x, yg, yd, yp, wg, wd, wp, fg, tm, final):
    B, T, D = x.shape
    tok = lambda w: pl.BlockSpec((None, tm, w), lambda b, t: (b, t, 0))
    const = lambda s: pl.BlockSpec(s, lambda b, t: (0, 0))
    return pl.pallas_call(
        functools.partial(_out_proj_kernel, final=final),
        grid=(B, T // tm),
        in_specs=[tok(D), tok(GLA_HP), tok(DSA_W), tok(POOL_W),
                  const((GLA_HP, D)), const((DSA_W, D)), const((POOL_W, D)), const((1, D))],
        out_specs=tok(D),
        out_shape=jax.ShapeDtypeStruct((B, T, D), f32),
        compiler_params=_params(),
        name="out_proj_final" if final else "out_proj",
    )(x, yg, yd, yp, wg, wd, wp, fg)


def _pad_heads(w, heads, d, slot=LANE):
    lead = w.shape[:-1]
    w = w.reshape(*lead, heads, d)
    w = jnp.pad(w, [(0, 0)] * len(lead) + [(0, 0), (0, slot - d)])
    return w.reshape(*lead, heads * slot)


def _split_cols(w):
    outs, off = [], 0
    for s in IN_SIZES:
        outs.append(w[..., off:off + s])
        off += s
    return outs


def _layer_weights(w_in, gate_w2, gate_b, gla_norm_g, pool_w, pool_scale, w_out):
    (gq, gk, gv, gz, gg, dq, dk, dv, dg, qi, ki, wi, pu, pg) = _split_cols(w_in)
    D = w_in.shape[0]
    gz_lanes = slice(GLA_DK, GLA_DK + GLA_GATE_RANK)
    gq_slab = _pad_heads(gq, GLA_HEADS, GLA_DK, GLA_QK_SLOT).at[:, gz_lanes].set(gz)
    wn = jnp.concatenate([
        gq_slab, _pad_heads(gk, GLA_HEADS, GLA_DK, GLA_QK_SLOT),
        _pad_heads(gv, GLA_HEADS, GLA_DV), _pad_heads(gg, GLA_HEADS, GLA_DV),
        dk, jnp.tile(ki, (1, IDX_HEADS)), dg, pu, pg], axis=1).astype(bf16)
    wt = jnp.concatenate([
        dq, dv, qi, jnp.pad(wi, ((0, 0), (0, SUBLANE - IDX_HEADS)))], axis=1).T.astype(bf16)
    w2 = jnp.zeros((GLA_QK_W, GLA_QK_W), f32).at[gz_lanes, :].set(
        _pad_heads(gate_w2, GLA_HEADS, GLA_DK, GLA_QK_SLOT)).astype(bf16)
    b2 = _pad_heads(gate_b[None, :], GLA_HEADS, GLA_DK, GLA_QK_SLOT)
    gn = jnp.pad(gla_norm_g[None, :], ((0, 0), (0, LANE - GLA_DV)))
    wbd = jax.scipy.linalg.block_diag(*[pool_w[g] for g in range(POOL_GROUPS)]).astype(bf16)
    sc = pool_scale[None, :]
    wg = _pad_heads(w_out[:GLA_W].T, GLA_HEADS, GLA_DV).T.astype(bf16)
    wd = w_out[GLA_W:GLA_W + DSA_W].astype(bf16)
    wp = w_out[GLA_W + DSA_W:].astype(bf16)
    return wn, wt, w2, b2, gn, wbd, sc, wg, wd, wp


def _t5_bucket_table(n):
    rel = np.arange(n)
    max_exact = REL_BUCKETS // 2
    relf = np.maximum(rel, 1).astype(np.float32)
    large = max_exact + (np.log(relf / np.float32(max_exact)) / np.float32(math.log(REL_MAX_DIST / max_exact))
                         * np.float32(REL_BUCKETS - max_exact)).astype(np.int32)
    large = np.minimum(large, REL_BUCKETS - 1)
    return np.where(rel < max_exact, rel, large)


def _near_bias(rel_bias, tq):
    bucket = _t5_bucket_table(3 * tq)
    assert np.all(bucket[tq + 1:] == REL_BUCKETS - 1)
    by_rel = (rel_bias[bucket] - rel_bias[REL_BUCKETS - 1][None, :]) * LOG2E
    n = 4 * tq
    u = np.arange(n)
    u = np.where(u < 2 * tq, u, u - n)
    brow = jnp.pad(by_rel[np.maximum(2 * tq + u, 0)].T, ((0, SUBLANE - DSA_HEADS), (0, 0)))
    bmax = jnp.max(by_rel, axis=0)
    bmax = jnp.broadcast_to(jnp.pad(bmax, (0, SUBLANE - DSA_HEADS))[:, None], (SUBLANE, LANE))
    return brow, bmax


def kernel(x, norm_g, w_in, gla_gate_w2, gla_gate_b, gla_norm_g, rel_bias, pool_w, pool_scale, w_out,
           final_norm_g):
    B, T, D = x.shape
    assert D == D_MODEL
    tm = min(512, T)
    tq = min(256, T)
    topk = min(TOPK_MAX, T // 4)
    assert T % tm == 0 and T % (2 * tq) == 0 and tq >= LANE and T % GLA_CHUNK == 0
    bias, bmax = _near_bias(rel_bias.astype(f32), tq)
    fg = final_norm_g[None, :]
    for l in range(DEPTH):
        wn, wt, w2, b2, gn, wbd, sc, wg, wd, wp = _layer_weights(
            w_in[l], gla_gate_w2[l], gla_gate_b[l], gla_norm_g[l], pool_w[l], pool_scale[l], w_out[l])
        (gq, gk, gv, gg, dk, kx, dg, pu, pg, dqT, dvT, qiT, wiT) = _in_proj(x, norm_g[l][None, :], wn, wt, tm)
        yg = _gla(gq, gk, gv, gg, w2, b2, gn, tm)
        yp = _pool(pu, pg, wbd, sc, tm)
        yd = _dsa(dqT, qiT, wiT, dk, kx, dvT, dg, bias, bmax, tq, topk)
        x = _out_proj(x, yg, yd, yp, wg, wd, wp, fg, tm, final=(l == DEPTH - 1))
    return x
```

```python
import functools
import math

import numpy as np
import jax
import jax.numpy as jnp
from jax import lax
from jax.experimental import pallas as pl
from jax.experimental.pallas import tpu as pltpu

D_MODEL = 1024
DEPTH = 4
EPS = 1e-6
GLA_HEADS = 4
GLA_DK = 48
GLA_DV = 96
GLA_GATE_RANK = 16
GLA_GATE_NORM = 16.0
GLA_CHUNK = 64
DSA_HEADS = 6
DSA_DH = 64
IDX_HEADS = 4
IDX_DIM = 32
TOPK_MAX = 256
POOL_GROUPS = 4
POOL_GC = 64
POOL_WINDOWS = (2, 4, 8, 16)
REL_BUCKETS = 32
REL_MAX_DIST = 128

GLA_W = GLA_HEADS * GLA_DV
DSA_W = DSA_HEADS * DSA_DH
POOL_W = POOL_GROUPS * POOL_GC
IN_SIZES = (GLA_HEADS * GLA_DK, GLA_HEADS * GLA_DK, GLA_W, GLA_GATE_RANK, GLA_W,
            DSA_W, DSA_W, DSA_W, DSA_W, IDX_HEADS * IDX_DIM, IDX_DIM, IDX_HEADS,
            POOL_W, POOL_W)

LANE = 128
SUBLANE = 8
VMEM_LIMIT_BYTES = 56 * 1024 * 1024

GLA_HP = GLA_HEADS * LANE
GLA_QK_SLOT = LANE // 2
GLA_QK_W = GLA_HEADS * GLA_QK_SLOT
assert GLA_DK + GLA_GATE_RANK <= GLA_QK_SLOT
POOL_HALO = 16
ONES_ROWS = 16
LOG2E = math.log2(math.e)
DSA_Q_SCALE = (DSA_DH ** -0.5) * LOG2E
SHIFT_MARGIN = 1.01
MIN_DENOM = 2.0 ** -60
MASK_GROUP = 4

_NAT = {}
_off = 0
for _name, _w in (("gq", GLA_QK_W), ("gk", GLA_QK_W), ("gv", GLA_HP), ("gg", GLA_HP),
                  ("dk", DSA_W), ("kx", LANE), ("dg", DSA_W), ("pu", POOL_W), ("pg", POOL_W)):
    _NAT[_name] = (_off, _off + _w)
    _off += _w
NAT_COLS = _off
_TR = {}
_off = 0
for _name, _w in (("dq", DSA_W), ("dv", DSA_W), ("qi", IDX_HEADS * IDX_DIM), ("wi", SUBLANE)):
    _TR[_name] = (_off, _off + _w)
    _off += _w
TR_ROWS = _off

INT_MIN = -2 ** 31
NEG_BIG = -1e30

f32 = jnp.float32
bf16 = jnp.bfloat16
i32 = jnp.int32


def _silu(x):
    return x * jax.nn.sigmoid(x)


def _dot(a, b):
    return jnp.dot(a, b, preferred_element_type=f32)


def _dot_nt(a, b):
    return lax.dot_general(a, b, (((1,), (1,)), ((), ())), preferred_element_type=f32)


def _dot_tn(a, b):
    return lax.dot_general(a, b, (((0,), (0,)), ((), ())), preferred_element_type=f32)


def _params():
    return pltpu.CompilerParams(dimension_semantics=("arbitrary", "arbitrary"),
                                vmem_limit_bytes=VMEM_LIMIT_BYTES)


_NAT_DTYPES = {"gq": f32, "gk": f32, "gv": f32, "gg": f32,
               "dk": bf16, "kx": bf16, "dg": f32, "pu": f32, "pg": f32}
_TR_DTYPES = {"dq": bf16, "dv": bf16, "qi": bf16, "wi": f32}
N_PROJ_OUTS = len(_NAT_DTYPES) + len(_TR_DTYPES)


def _project(x, g_ref, wn_ref, wt_ref, out_refs):
    h = x * lax.rsqrt(jnp.mean(x * x, axis=-1, keepdims=True) + EPS) * g_ref[...]
    hb = h.astype(bf16)
    nat_refs, tr_refs = out_refs[:len(_NAT_DTYPES)], out_refs[len(_NAT_DTYPES):]
    for name, ref in zip(_NAT_DTYPES, nat_refs):
        lo, hi = _NAT[name]
        ref[...] = _dot(hb, wn_ref[:, lo:hi]).astype(ref.dtype)
    for name, ref in zip(_TR_DTYPES, tr_refs):
        lo, hi = _TR[name]
        y = _dot_nt(wt_ref[lo:hi, :], hb)
        if name == "dq":
            y = y * DSA_Q_SCALE
        ref[...] = y.astype(ref.dtype)


def _proj_outs(B, T, tm):
    out_shape, out_specs = [], []
    for name, dt in _NAT_DTYPES.items():
        w = _NAT[name][1] - _NAT[name][0]
        out_shape.append(jax.ShapeDtypeStruct((B, T, w), dt))
        out_specs.append(pl.BlockSpec((None, tm, w), lambda b, t: (b, t, 0)))
    for name, dt in _TR_DTYPES.items():
        w = _TR[name][1] - _TR[name][0]
        out_shape.append(jax.ShapeDtypeStruct((B, w, T), dt))
        out_specs.append(pl.BlockSpec((None, w, tm), lambda b, t: (b, 0, t)))
    return out_shape, out_specs


def _in_proj_kernel(x_ref, g_ref, wn_ref, wt_ref, *out_refs):
    _project(x_ref[...], g_ref, wn_ref, wt_ref, out_refs)


def _in_proj(x, g, wn, wt, tm):
    B, T, D = x.shape
    out_shape, out_specs = _proj_outs(B, T, tm)
    return pl.pallas_call(
        _in_proj_kernel,
        grid=(B, T // tm),
        in_specs=[pl.BlockSpec((None, tm, D), lambda b, t: (b, t, 0)),
                  pl.BlockSpec((1, D), lambda b, t: (0, 0)),
                  pl.BlockSpec((D, NAT_COLS), lambda b, t: (0, 0)),
                  pl.BlockSpec((TR_ROWS, D), lambda b, t: (0, 0))],
        out_specs=out_specs,
        out_shape=out_shape,
        compiler_params=_params(),
        name="in_proj",
    )(x, g, wn, wt)


def _gla_kernel(gq_ref, gk_ref, gv_ref, gg_ref, w2_ref, b2_ref, gn_ref, out_ref,
                st_ref, glog_ref):
    C = GLA_CHUNK
    n_chunks = gq_ref.shape[0] // C

    @pl.when(pl.program_id(1) == 0)
    def _():
        st_ref[...] = jnp.zeros_like(st_ref)

    z = _dot(gq_ref[...].astype(bf16), w2_ref[...]) + b2_ref[...]
    b = (jnp.minimum(z, 0.0) - jnp.log1p(jnp.exp(-jnp.abs(z)))) / GLA_GATE_NORM
    pos = lax.broadcasted_iota(i32, b.shape, 0) & (C - 1)
    shift = 1
    while shift < C:
        b = b + jnp.where(pos >= shift, pltpu.roll(b, shift, 0), 0.0)
        shift *= 2
    glog_ref[...] = b

    causal = lax.broadcasted_iota(i32, (C, C), 1) <= lax.broadcasted_iota(i32, (C, C), 0)
    gn = gn_ref[...]
    heads = range(GLA_HEADS)
    pairs = range(GLA_HEADS // 2)
    cols = [slice(h * LANE, (h + 1) * LANE) for h in heads]
    pair_cols = [slice(g * LANE, (g + 1) * LANE) for g in pairs]
    lane_half = lax.broadcasted_iota(i32, (C, LANE), 1) // GLA_QK_SLOT
    own_half = [lane_half == 0, lane_half == 1]

    group = 2 if n_chunks % 2 == 0 else 1

    def chunks(c, carry):
        rs = [pl.ds(pl.multiple_of((c * group + t) * C, C), C) for t in range(group)]
        bs = [[glog_ref[r, ps] for ps in pair_cols] for r in rs]
        b_last = [[bb[C - 1:C, :] for bb in bt] for bt in bs]
        ks = [[gk_ref[r, ps] for ps in pair_cols] for r in rs]
        vs = [[gv_ref[r, cs].astype(bf16) for cs in cols] for r in rs]
        qe_pair = [[gq_ref[rs[t], pair_cols[g]] * jnp.exp(bs[t][g]) * (GLA_DK ** -0.5) for g in pairs]
                   for t in range(group)]
        qe = [[jnp.where(own_half[h % 2], qe_pair[t][h // 2], 0.0).astype(bf16) for h in heads]
              for t in range(group)]
        ke = [[(ks[t][g] * jnp.exp(-bs[t][g])).astype(bf16) for g in pairs] for t in range(group)]
        kd = [[(ks[t][g] * jnp.exp(b_last[t][g] - bs[t][g])).astype(bf16) for g in pairs] for t in range(group)]
        a = [[_dot_nt(qe[t][h], ke[t][h // 2]) for h in heads] for t in range(group)]
        u = [[_dot_tn(vs[t][h], kd[t][h // 2]) for h in heads] for t in range(group)]
        st = [st_ref[h] for h in heads]
        inter = []
        for t in range(group):
            inter.append([_dot_nt(qe[t][h], st[h].astype(bf16)) for h in heads])
            st = [st[h] * jnp.exp(b_last[t][h // 2]) + u[t][h] for h in heads]
        for h in heads:
            st_ref[h] = st[h]
        for t in range(group):
            for h in heads:
                o = _dot(jnp.where(causal, a[t][h], 0.0).astype(bf16), vs[t][h]) + inter[t][h]
                ms = jnp.sum(o * o, axis=-1, keepdims=True) * (1.0 / GLA_DV)
                y = o * lax.rsqrt(ms + EPS) * gn * _silu(gg_ref[rs[t], cols[h]])
                out_ref[rs[t], cols[h]] = y.astype(out_ref.dtype)
        return carry

    lax.fori_loop(0, n_chunks // group, chunks, 0)


def _gla(gq, gk, gv, gg, w2, b2, gn, tg):
    B, T, _ = gq.shape
    tok = lambda w: pl.BlockSpec((None, tg, w), lambda b, t: (b, t, 0))
    const = lambda s: pl.BlockSpec(s, lambda b, t: (0, 0))
    return pl.pallas_call(
        _gla_kernel,
        grid=(B, T // tg),
        in_specs=[tok(GLA_QK_W), tok(GLA_QK_W), tok(GLA_HP), tok(GLA_HP),
                  const((GLA_QK_W, GLA_QK_W)), const((1, GLA_QK_W)), const((1, LANE))],
        out_specs=tok(GLA_HP),
        out_shape=jax.ShapeDtypeStruct((B, T, GLA_HP), bf16),
        scratch_shapes=[pltpu.VMEM((GLA_HEADS, LANE, LANE), f32),
                        pltpu.VMEM((tg, GLA_QK_W), f32)],
        compiler_params=_params(),
        name="gla",
    )(gq, gk, gv, gg, w2, b2, gn)


def _pool_mixer(pu_ref, pg_ref, w_ref, sc_ref, halo_ref):
    tp = pu_ref.shape[0]
    H = POOL_HALO

    @pl.when(pl.program_id(1) == 0)
    def _():
        halo_ref[...] = jnp.zeros_like(halo_ref)

    u = pu_ref[...]
    ext = jnp.concatenate([halo_ref[...], u], axis=0)
    halo_ref[...] = u[tp - H:, :]
    s2 = ext + pltpu.roll(ext, 1, 0)
    s4 = s2 + pltpu.roll(s2, 2, 0)
    s8 = s4 + pltpu.roll(s4, 4, 0)
    s16 = s8 + pltpu.roll(s8, 8, 0)
    lane = lax.broadcasted_iota(i32, (tp, POOL_W), 1)
    grp = lane // POOL_GC
    sw = jnp.where(grp == 0, s2[H:], jnp.where(grp == 1, s4[H:], jnp.where(grp == 2, s8[H:], s16[H:])))
    win = jnp.where(grp == 0, POOL_WINDOWS[0],
                    jnp.where(grp == 1, POOL_WINDOWS[1],
                              jnp.where(grp == 2, POOL_WINDOWS[2], POOL_WINDOWS[3])))
    t = pl.program_id(1) * tp + lax.broadcasted_iota(i32, (tp, POOL_W), 0)
    cnt = jnp.minimum(t + 1, win).astype(f32)
    pooled = sw / cnt - u
    y = _dot(pooled.astype(bf16), w_ref[...]) * sc_ref[...]
    return (y * _silu(pg_ref[...])).astype(bf16)


def _mix_kernel(x_ref, yg_ref, yd_ref, pu_ref, pg_ref, wg_ref, wd_ref, wp_ref, wbd_ref, sc_ref, g_ref,
                *rest, project):
    halo_ref = rest[-1]
    yp = _pool_mixer(pu_ref, pg_ref, wbd_ref, sc_ref, halo_ref)
    x = x_ref[...] + _dot(yg_ref[...], wg_ref[...]) + _dot(yd_ref[...], wd_ref[...]) + _dot(yp, wp_ref[...])
    if project:
        wn_ref, wt_ref, xo_ref = rest[:3]
        xo_ref[...] = x
        _project(x, g_ref, wn_ref, wt_ref, rest[3:-1])
    else:
        rest[0][...] = x * lax.rsqrt(jnp.mean(x * x, axis=-1, keepdims=True) + EPS) * g_ref[...]


def _mix(x, yg, yd, pu, pg, wg, wd, wp, wbd, sc, g, wn=None, wt=None, *, tm):
    B, T, D = x.shape
    project = wn is not None
    tok = lambda w: pl.BlockSpec((None, tm, w), lambda b, t: (b, t, 0))
    const = lambda s: pl.BlockSpec(s, lambda b, t: (0, 0), pipeline_mode=pl.Buffered(1))
    in_specs = [tok(D), tok(GLA_HP), tok(DSA_W), tok(POOL_W), tok(POOL_W),
                const((GLA_HP, D)), const((DSA_W, D)), const((POOL_W, D)), const((POOL_W, POOL_W)),
                const((1, POOL_W)), const((1, D))]
    args = [x, yg, yd, pu, pg, wg, wd, wp, wbd, sc, g]
    out_shape, out_specs = [jax.ShapeDtypeStruct((B, T, D), f32)], [tok(D)]
    if project:
        in_specs += [const((D, NAT_COLS)), const((TR_ROWS, D))]
        args += [wn, wt]
        proj_shape, proj_specs = _proj_outs(B, T, tm)
        out_shape += proj_shape
        out_specs += proj_specs
    out = pl.pallas_call(
        functools.partial(_mix_kernel, project=project),
        grid=(B, T // tm),
        in_specs=in_specs,
        out_specs=out_specs,
        out_shape=out_shape,
        scratch_shapes=[pltpu.VMEM((POOL_HALO, POOL_W), f32)],
        compiler_params=_params(),
        name="mix_project" if project else "mix_final",
    )(*args)
    return (out[0], out[1:]) if project else out[0]


def _dsa_kernel(qT_ref, qiT_ref, wiT_ref, k_ref, kx_ref, vT_ref, dg_ref, brow_ref, bmax_ref, out_ref,
                keys_ref, planes_ref, active_ref, qm_ref, qim_ref, acc_ref, kmax_ref, bias_ref, *, topk):
    tq = qT_ref.shape[1]
    tk = tq
    qb = pl.program_id(1)
    n_blocks = qb + 1

    def rows(kb):
        return pl.ds(pl.multiple_of(kb * tk, tk), tk)

    @pl.when(qb == 0)
    def _():
        def body(kb, mx):
            kk = k_ref[rows(kb), :].astype(f32)
            n2 = jnp.sum(kk * kk, axis=1, keepdims=True)
            return jnp.maximum(mx, jnp.max(n2, axis=0, keepdims=True))
        mx = lax.fori_loop(0, k_ref.shape[0] // tk, body, jnp.zeros((1, 1), f32))
        kmax_ref[...] = jnp.broadcast_to(jnp.sqrt(mx), kmax_ref.shape)

    prow = lax.broadcasted_iota(i32, (LANE, tq), 0)
    for h in range(DSA_HEADS):
        pair = qT_ref[(h // 2) * LANE:(h // 2 + 1) * LANE, :]
        qm_ref[h // 2, :, (h % 2) * tq:(h % 2 + 1) * tq] = jnp.where(
            (prow // DSA_DH) == (h % 2), pair, jnp.zeros_like(pair))
    qi = qiT_ref[...]
    for h in range(IDX_HEADS):
        qim_ref[:, h * tq:(h + 1) * tq] = jnp.where((prow // IDX_DIM) == h, qi, jnp.zeros_like(qi))
    w = wiT_ref[...] * ((IDX_DIM ** -0.5) * (IDX_HEADS ** -0.5))

    def score_keys(r):
        d = _dot(kx_ref[r, :], qim_ref[...])
        s = jnp.zeros((d.shape[0], tq), f32)
        for h in range(IDX_HEADS):
            s = s + jnp.maximum(d[:, h * tq:(h + 1) * tq], 0.0) * w[h:h + 1, :]
        bits = lax.bitcast_convert_type(s, i32)
        return jnp.where(bits < 0, INT_MIN - bits, bits)

    def store_keys(kb, keys):
        keys_ref[rows(kb), :] = keys
        u = keys ^ INT_MIN
        words = [u[SUBLANE * k:SUBLANE * (k + 1), :] for k in range(32)]
        j, m = 16, 0x0000FFFF
        while j:
            for k in range(32):
                if k & j == 0:
                    t = (lax.shift_right_logical(words[k], jnp.int32(j)) ^ words[k + j]) & m
                    words[k + j] = words[k + j] ^ t
                    words[k] = words[k] ^ lax.shift_left(t, jnp.int32(j))
            j >>= 1
            m ^= m << j
        r8 = pl.ds(pl.multiple_of(kb * SUBLANE, SUBLANE), SUBLANE)
        for b in range(32):
            planes_ref[b, r8, :] = words[b]

    def score_body(i, c, span):
        keys = score_keys(pl.ds(pl.multiple_of(i * span, span), span))
        for part in range(span // tk):
            store_keys(i * (span // tk) + part, keys[part * tk:(part + 1) * tk, :])
        return c

    @pl.when((pl.program_id(0) == 0) & (qb == 0))
    def _():
        planes_ref[...] = jnp.zeros_like(planes_ref)
        for h in range(DSA_HEADS):
            for c in range(bias_ref.shape[1] // tk):
                x = jnp.broadcast_to(brow_ref[h:h + 1, :], (tk, brow_ref.shape[1]))
                x = pltpu.roll(x, c * tk, 1, stride=1, stride_axis=0)
                bias_ref[h, c * tk:(c + 1) * tk, :] = x[:, :tq]

    lax.fori_loop(0, qb // 2, functools.partial(score_body, span=2 * tk), 0)
    lax.fori_loop(2 * (qb // 2), qb, functools.partial(score_body, span=tk), 0)
    krow = lax.broadcasted_iota(i32, (tk, tq), 0)
    qcol = lax.broadcasted_iota(i32, (tk, tq), 1)
    store_keys(qb, jnp.where(krow <= qcol, score_keys(rows(qb)), INT_MIN))

    for extra in range(MASK_GROUP - 1):
        keys_ref[rows(n_blocks + extra), :] = jnp.full((tk, tq), INT_MIN, i32)

    n_rows = planes_ref.shape[1]
    prow8 = lax.broadcasted_iota(i32, (n_rows, tq), 0)
    active_ref[...] = jnp.where(prow8 < n_blocks * SUBLANE, -1, 0)

    half = n_rows // 2
    upper = n_blocks * SUBLANE > half

    def count_set(b, r0):
        ones = lax.population_count(active_ref[r0:r0 + half, :] & planes_ref[b, r0:r0 + half, :])
        return jnp.sum(jnp.sum(ones.reshape(half // SUBLANE, SUBLANE, tq), axis=0), axis=0, keepdims=True)

    def narrow(b, r0, flip):
        active_ref[r0:r0 + half, :] = active_ref[r0:r0 + half, :] & (planes_ref[b, r0:r0 + half, :] ^ flip)

    def radix_step(i, carry):
        n_gt, tau_u = carry
        b = 31 - i
        n_set = count_set(b, 0) + lax.cond(upper, lambda: count_set(b, half),
                                           lambda: jnp.zeros((1, tq), i32))
        take = n_gt + n_set >= topk
        flip = jnp.where(take, 0, -1)
        narrow(b, 0, flip)

        @pl.when(upper)
        def _():
            narrow(b, half, flip)

        return (jnp.where(take, n_gt, n_gt + n_set),
                tau_u | jnp.where(take, lax.shift_left(jnp.int32(1), b), 0))

    n_gt, tau_u = lax.fori_loop(0, 32, radix_step, (jnp.zeros((1, tq), i32), jnp.zeros((1, tq), i32)))
    tau = tau_u ^ INT_MIN
    need = jnp.where(tau == INT_MIN, 0, topk - n_gt).astype(f32)

    tri = (lax.broadcasted_iota(i32, (tk, tk), 1) <= lax.broadcasted_iota(i32, (tk, tk), 0)).astype(bf16)

    def mask_body(g, run):
        ks = [keys_ref[rows(MASK_GROUP * g + part), :] for part in range(MASK_GROUP)]
        eqs = [k == tau for k in ks]
        cnt = [_dot(tri, e.astype(bf16)) for e in eqs]
        for part in range(MASK_GROUP):
            pre = cnt[part] + run
            sel = (ks[part] > tau) | (eqs[part] & (pre <= need))
            keys_ref[rows(MASK_GROUP * g + part), :] = lax.bitcast_convert_type(
                jnp.where(sel, 0.0, NEG_BIG).astype(f32), i32)
            run = pre[tk - 1:tk, :]
        return run

    lax.fori_loop(0, (n_blocks + MASK_GROUP - 1) // MASK_GROUP, mask_body, jnp.zeros((1, tq), f32))

    def tile_ops(i, span, near):
        r = pl.ds(pl.multiple_of(i * span, span), span)
        ones = jnp.ones((ONES_ROWS, span), bf16)
        madd = lax.bitcast_convert_type(keys_ref[r, :], f32)
        dots = [_dot(k_ref[r, g * LANE:(g + 1) * LANE], qm_ref[g]) for g in range(DSA_HEADS // 2)]

        def logits(h):
            lg = dots[h // 2][:, (h % 2) * tq:(h % 2 + 1) * tq] + madd
            if near:
                j = i - (qb - 2)
                lg = lg + bias_ref[h, pl.ds(pl.multiple_of(j * tk, tk), tk), :]
            return lg

        def v1(h):
            return jnp.concatenate([vT_ref[h * DSA_DH:(h + 1) * DSA_DH, r], ones], axis=0)

        return logits, v1

    def attend_online(i, ms, span, near):
        logits, v1 = tile_ops(i, span, near)
        new_ms = []
        for h in range(DSA_HEADS):
            lg = logits(h)
            m_new = jnp.maximum(ms[h], jnp.max(lg, axis=0, keepdims=True))
            alpha = jnp.exp2(ms[h] - m_new)
            acc_ref[h] = alpha * acc_ref[h] + _dot(v1(h), jnp.exp2(lg - m_new).astype(bf16))
            new_ms.append(m_new)
        return tuple(new_ms)

    def attend_fixed(i, shift, span, near):
        logits, v1 = tile_ops(i, span, near)
        for h in range(DSA_HEADS):
            acc_ref[h] = acc_ref[h] + _dot(v1(h), jnp.exp2(logits(h) - shift[h]).astype(bf16))
        return shift

    def sweep(body, carry):
        n_far_pairs = jnp.maximum(qb - 1, 0) // 2
        n_far_quads = n_far_pairs // 2
        acc_ref[...] = jnp.zeros_like(acc_ref)
        carry = lax.fori_loop(0, n_far_quads, functools.partial(body, span=4 * tk, near=False), carry)
        carry = lax.fori_loop(2 * n_far_quads, n_far_pairs, functools.partial(body, span=2 * tk, near=False), carry)
        return lax.fori_loop(2 * n_far_pairs, n_blocks, functools.partial(body, span=tk, near=True), carry)

    q_norm = []
    for g in range(DSA_HEADS // 2):
        qf = qm_ref[g].astype(f32)
        n2 = jnp.sum(qf * qf, axis=0, keepdims=True)
        q_norm += [jnp.sqrt(n2[:, :tq]), jnp.sqrt(n2[:, tq:])]
    shift = tuple(q_norm[h] * (kmax_ref[0:1, 0:1] * SHIFT_MARGIN) + bmax_ref[h:h + 1, 0:1]
                  for h in range(DSA_HEADS))
    sweep(attend_fixed, shift)
    denom = acc_ref[0, DSA_DH:DSA_DH + 1, :]
    for h in range(1, DSA_HEADS):
        denom = jnp.minimum(denom, acc_ref[h, DSA_DH:DSA_DH + 1, :])
    accurate = jnp.min(denom) >= MIN_DENOM

    @pl.when(jnp.logical_not(accurate))
    def _():
        sweep(attend_online, tuple(jnp.full((1, tq), NEG_BIG, f32) for _ in range(DSA_HEADS)))

    oT = jnp.concatenate([acc_ref[h, :DSA_DH, :] / acc_ref[h, DSA_DH:DSA_DH + 1, :]
                          for h in range(DSA_HEADS)], axis=0)
    out_ref[...] = (oT.T * _silu(dg_ref[...])).astype(out_ref.dtype)


def _dsa(dqT, qiT, wiT, dk, kx, dvT, dg, bias, bmax, tq, topk):
    B, T, _ = dk.shape
    whole = lambda shape, imap: pl.BlockSpec(shape, imap, pipeline_mode=pl.Buffered(1))
    return pl.pallas_call(
        functools.partial(_dsa_kernel, topk=topk),
        grid=(B, T // tq),
        in_specs=[pl.BlockSpec((None, DSA_W, tq), lambda b, q: (b, 0, q)),
                  pl.BlockSpec((None, IDX_HEADS * IDX_DIM, tq), lambda b, q: (b, 0, q)),
                  pl.BlockSpec((None, SUBLANE, tq), lambda b, q: (b, 0, q)),
                  whole((None, T, DSA_W), lambda b, q: (b, 0, 0)),
                  whole((None, T, LANE), lambda b, q: (b, 0, 0)),
                  whole((None, DSA_W, T), lambda b, q: (b, 0, 0)),
                  pl.BlockSpec((None, tq, DSA_W), lambda b, q: (b, q, 0)),
                  pl.BlockSpec((SUBLANE, 4 * tq), lambda b, q: (0, 0)),
                  pl.BlockSpec((SUBLANE, LANE), lambda b, q: (0, 0))],
        out_specs=pl.BlockSpec((None, tq, DSA_W), lambda b, q: (b, q, 0)),
        out_shape=jax.ShapeDtypeStruct((B, T, DSA_W), bf16),
        scratch_shapes=[pltpu.VMEM((T + (MASK_GROUP - 1) * tq, tq), i32),
                        pltpu.VMEM((32, T // 32, tq), i32),
                        pltpu.VMEM((T // 32, tq), i32),
                        pltpu.VMEM((DSA_HEADS // 2, LANE, 2 * tq), bf16),
                        pltpu.VMEM((LANE, IDX_HEADS * tq), bf16),
                        pltpu.VMEM((DSA_HEADS, DSA_DH + ONES_ROWS, tq), f32),
                        pltpu.VMEM((SUBLANE, LANE), f32),
                        pltpu.VMEM((DSA_HEADS, 3 * tq, tq), f32)],
        compiler_params=_params(),
        name="dsa",
    )(dqT, qiT, wiT, dk, kx, dvT, dg, bias, bmax)


def _pad_heads(w, heads, d, slot=LANE):
    lead = w.shape[:-1]
    w = w.reshape(*lead, heads, d)
    w = jnp.pad(w, [(0, 0)] * len(lead) + [(0, 0), (0, slot - d)])
    return w.reshape(*lead, heads * slot)


def _split_cols(w):
    outs, off = [], 0
    for s in IN_SIZES:
        outs.append(w[..., off:off + s])
        off += s
    return outs


def _layer_weights(w_in, gate_w2, gate_b, gla_norm_g, pool_w, pool_scale, w_out):
    (gq, gk, gv, gz, gg, dq, dk, dv, dg, qi, ki, wi, pu, pg) = _split_cols(w_in)
    D = w_in.shape[0]
    gz_lanes = slice(GLA_DK, GLA_DK + GLA_GATE_RANK)
    gq_slab = _pad_heads(gq, GLA_HEADS, GLA_DK, GLA_QK_SLOT).at[:, gz_lanes].set(gz)
    wn = jnp.concatenate([
        gq_slab, _pad_heads(gk, GLA_HEADS, GLA_DK, GLA_QK_SLOT),
        _pad_heads(gv, GLA_HEADS, GLA_DV), _pad_heads(gg, GLA_HEADS, GLA_DV),
        dk, jnp.tile(ki, (1, IDX_HEADS)), dg, pu, pg], axis=1).astype(bf16)
    wt = jnp.concatenate([
        dq, dv, qi, jnp.pad(wi, ((0, 0), (0, SUBLANE - IDX_HEADS)))], axis=1).T.astype(bf16)
    w2 = jnp.zeros((GLA_QK_W, GLA_QK_W), f32).at[gz_lanes, :].set(
        _pad_heads(gate_w2, GLA_HEADS, GLA_DK, GLA_QK_SLOT)).astype(bf16)
    b2 = _pad_heads(gate_b[None, :], GLA_HEADS, GLA_DK, GLA_QK_SLOT)
    gn = jnp.pad(gla_norm_g[None, :], ((0, 0), (0, LANE - GLA_DV)))
    wbd = jax.scipy.linalg.block_diag(*[pool_w[g] for g in range(POOL_GROUPS)]).astype(bf16)
    sc = pool_scale[None, :]
    wg = _pad_heads(w_out[:GLA_W].T, GLA_HEADS, GLA_DV).T.astype(bf16)
    wd = w_out[GLA_W:GLA_W + DSA_W].astype(bf16)
    wp = w_out[GLA_W + DSA_W:].astype(bf16)
    return wn, wt, w2, b2, gn, wbd, sc, wg, wd, wp


def _t5_bucket_table(n):
    rel = np.arange(n)
    max_exact = REL_BUCKETS // 2
    relf = np.maximum(rel, 1).astype(np.float32)
    large = max_exact + (np.log(relf / np.float32(max_exact)) / np.float32(math.log(REL_MAX_DIST / max_exact))
                         * np.float32(REL_BUCKETS - max_exact)).astype(np.int32)
    large = np.minimum(large, REL_BUCKETS - 1)
    return np.where(rel < max_exact, rel, large)


def _near_bias(rel_bias, tq):
    bucket = _t5_bucket_table(3 * tq)
    assert np.all(bucket[tq + 1:] == REL_BUCKETS - 1)
    by_rel = (rel_bias[bucket] - rel_bias[REL_BUCKETS - 1][None, :]) * LOG2E
    n = 4 * tq
    u = np.arange(n)
    u = np.where(u < 2 * tq, u, u - n)
    brow = jnp.pad(by_rel[np.maximum(2 * tq + u, 0)].T, ((0, SUBLANE - DSA_HEADS), (0, 0)))
    bmax = jnp.max(by_rel, axis=0)
    bmax = jnp.broadcast_to(jnp.pad(bmax, (0, SUBLANE - DSA_HEADS))[:, None], (SUBLANE, LANE))
    return brow, bmax


def kernel(x, norm_g, w_in, gla_gate_w2, gla_gate_b, gla_norm_g, rel_bias, pool_w, pool_scale, w_out,
           final_norm_g):
    B, T, D = x.shape
    assert D == D_MODEL
    tm = min(512, T)
    tq = min(256, T)
    topk = min(TOPK_MAX, T // 4)
    assert T % tm == 0 and T % (2 * tq) == 0 and tq >= LANE and T % GLA_CHUNK == 0
    bias, bmax = _near_bias(rel_bias.astype(f32), tq)
    weights = [_layer_weights(w_in[l], gla_gate_w2[l], gla_gate_b[l], gla_norm_g[l], pool_w[l],
                              pool_scale[l], w_out[l]) for l in range(DEPTH)]
    proj = _in_proj(x, norm_g[0][None, :], weights[0][0], weights[0][1], tm)
    for l in range(DEPTH):
        _, _, w2, b2, gn, wbd, sc, wg, wd, wp = weights[l]
        (gq, gk, gv, gg, dk, kx, dg, pu, pg, dqT, dvT, qiT, wiT) = proj
        yg = _gla(gq, gk, gv, gg, w2, b2, gn, tm)
        yd = _dsa(dqT, qiT, wiT, dk, kx, dvT, dg, bias, bmax, tq, topk)
        if l + 1 < DEPTH:
            x, proj = _mix(x, yg, yd, pu, pg, wg, wd, wp, wbd, sc, norm_g[l + 1][None, :],
                           weights[l + 1][0], weights[l + 1][1], tm=tm)
        else:
            x = _mix(x, yg, yd, pu, pg, wg, wd, wp, wbd, sc, final_norm_g[None, :], tm=tm)
    return x
```

```python
import functools
import math

import numpy as np
import jax
import jax.numpy as jnp
from jax import lax
from jax.experimental import pallas as pl
from jax.experimental.pallas import tpu as pltpu

D_MODEL = 1024
DEPTH = 4
EPS = 1e-6
GLA_HEADS = 4
GLA_DK = 48
GLA_DV = 96
GLA_GATE_RANK = 16
GLA_GATE_NORM = 16.0
GLA_CHUNK = 64
DSA_HEADS = 6
DSA_DH = 64
IDX_HEADS = 4
IDX_DIM = 32
TOPK_MAX = 256
POOL_GROUPS = 4
POOL_GC = 64
POOL_WINDOWS = (2, 4, 8, 16)
REL_BUCKETS = 32
REL_MAX_DIST = 128

GLA_W = GLA_HEADS * GLA_DV
DSA_W = DSA_HEADS * DSA_DH
POOL_W = POOL_GROUPS * POOL_GC
IN_SIZES = (GLA_HEADS * GLA_DK, GLA_HEADS * GLA_DK, GLA_W, GLA_GATE_RANK, GLA_W,
            DSA_W, DSA_W, DSA_W, DSA_W, IDX_HEADS * IDX_DIM, IDX_DIM, IDX_HEADS,
            POOL_W, POOL_W)

LANE = 128
SUBLANE = 8
VMEM_LIMIT_BYTES = 56 * 1024 * 1024

GLA_HP = GLA_HEADS * LANE
GLA_QK_SLOT = LANE // 2
GLA_QK_W = GLA_HEADS * GLA_QK_SLOT
assert GLA_DK + GLA_GATE_RANK <= GLA_QK_SLOT
POOL_HALO = 16
ONES_ROWS = 16
LOG2E = math.log2(math.e)
DSA_Q_SCALE = (DSA_DH ** -0.5) * LOG2E
SHIFT_MARGIN = 1.01
MIN_DENOM = 2.0 ** -60
MASK_GROUP = 4

_NAT = {}
_off = 0
for _name, _w in (("gq", GLA_QK_W), ("gk", GLA_QK_W), ("gv", GLA_HP), ("gg", GLA_HP),
                  ("dk", DSA_W), ("kx", LANE), ("dg", DSA_W), ("pu", POOL_W), ("pg", POOL_W)):
    _NAT[_name] = (_off, _off + _w)
    _off += _w
NAT_COLS = _off
_TR = {}
_off = 0
for _name, _w in (("dq", DSA_W), ("dv", DSA_W), ("qi", IDX_HEADS * IDX_DIM), ("wi", SUBLANE)):
    _TR[_name] = (_off, _off + _w)
    _off += _w
TR_ROWS = _off

INT_MIN = -2 ** 31
NEG_BIG = -1e30

f32 = jnp.float32
bf16 = jnp.bfloat16
i32 = jnp.int32


def _silu(x):
    return x * jax.nn.sigmoid(x)


def _dot(a, b):
    return jnp.dot(a, b, preferred_element_type=f32)


def _dot_nt(a, b):
    return lax.dot_general(a, b, (((1,), (1,)), ((), ())), preferred_element_type=f32)


def _dot_tn(a, b):
    return lax.dot_general(a, b, (((0,), (0,)), ((), ())), preferred_element_type=f32)


def _params():
    return pltpu.CompilerParams(dimension_semantics=("arbitrary", "arbitrary"),
                                vmem_limit_bytes=VMEM_LIMIT_BYTES)


_NAT_DTYPES = {"gq": f32, "gk": f32, "gv": f32, "gg": f32,
               "dk": bf16, "kx": bf16, "dg": f32, "pu": f32, "pg": f32}
_TR_DTYPES = {"dq": bf16, "dv": bf16, "qi": bf16, "wi": f32}
N_PROJ_OUTS = len(_NAT_DTYPES) + len(_TR_DTYPES)


def _project(x, g_ref, wn_ref, wt_ref, out_refs):
    h = x * lax.rsqrt(jnp.mean(x * x, axis=-1, keepdims=True) + EPS) * g_ref[...]
    hb = h.astype(bf16)
    nat_refs, tr_refs = out_refs[:len(_NAT_DTYPES)], out_refs[len(_NAT_DTYPES):]
    for name, ref in zip(_NAT_DTYPES, nat_refs):
        lo, hi = _NAT[name]
        ref[...] = _dot(hb, wn_ref[:, lo:hi]).astype(ref.dtype)
    for name, ref in zip(_TR_DTYPES, tr_refs):
        lo, hi = _TR[name]
        y = _dot_nt(wt_ref[lo:hi, :], hb)
        if name == "dq":
            y = y * DSA_Q_SCALE
        ref[...] = y.astype(ref.dtype)


def _proj_outs(B, T, tm):
    out_shape, out_specs = [], []
    for name, dt in _NAT_DTYPES.items():
        w = _NAT[name][1] - _NAT[name][0]
        out_shape.append(jax.ShapeDtypeStruct((B, T, w), dt))
        out_specs.append(pl.BlockSpec((None, tm, w), lambda b, t: (b, t, 0)))
    for name, dt in _TR_DTYPES.items():
        w = _TR[name][1] - _TR[name][0]
        out_shape.append(jax.ShapeDtypeStruct((B, w, T), dt))
        out_specs.append(pl.BlockSpec((None, w, tm), lambda b, t: (b, 0, t)))
    return out_shape, out_specs


def _in_proj_kernel(x_ref, g_ref, wn_ref, wt_ref, *out_refs):
    _project(x_ref[...], g_ref, wn_ref, wt_ref, out_refs)


def _in_proj(x, g, wn, wt, tm):
    B, T, D = x.shape
    out_shape, out_specs = _proj_outs(B, T, tm)
    return pl.pallas_call(
        _in_proj_kernel,
        grid=(B, T // tm),
        in_specs=[pl.BlockSpec((None, tm, D), lambda b, t: (b, t, 0)),
                  pl.BlockSpec((1, D), lambda b, t: (0, 0)),
                  pl.BlockSpec((D, NAT_COLS), lambda b, t: (0, 0)),
                  pl.BlockSpec((TR_ROWS, D), lambda b, t: (0, 0))],
        out_specs=out_specs,
        out_shape=out_shape,
        compiler_params=_params(),
        name="in_proj",
    )(x, g, wn, wt)


def _gla_kernel(gq_ref, gk_ref, gv_ref, gg_ref, w2_ref, b2_ref, gn_ref, out_ref,
                st_ref, glog_ref):
    C = GLA_CHUNK
    n_chunks = gq_ref.shape[0] // C

    @pl.when(pl.program_id(1) == 0)
    def _():
        st_ref[...] = jnp.zeros_like(st_ref)

    z = _dot(gq_ref[...].astype(bf16), w2_ref[...]) + b2_ref[...]
    b = (jnp.minimum(z, 0.0) - jnp.log1p(jnp.exp(-jnp.abs(z)))) / GLA_GATE_NORM
    pos = lax.broadcasted_iota(i32, b.shape, 0) & (C - 1)
    shift = 1
    while shift < C:
        b = b + jnp.where(pos >= shift, pltpu.roll(b, shift, 0), 0.0)
        shift *= 2
    glog_ref[...] = b

    causal = lax.broadcasted_iota(i32, (C, C), 1) <= lax.broadcasted_iota(i32, (C, C), 0)
    gn = gn_ref[...]
    heads = range(GLA_HEADS)
    pairs = range(GLA_HEADS // 2)
    cols = [slice(h * LANE, (h + 1) * LANE) for h in heads]
    pair_cols = [slice(g * LANE, (g + 1) * LANE) for g in pairs]
    lane_half = lax.broadcasted_iota(i32, (C, LANE), 1) // GLA_QK_SLOT
    own_half = [lane_half == 0, lane_half == 1]

    group = 2 if n_chunks % 2 == 0 else 1

    def chunks(c, carry):
        rs = [pl.ds(pl.multiple_of((c * group + t) * C, C), C) for t in range(group)]
        bs = [[glog_ref[r, ps] for ps in pair_cols] for r in rs]
        b_last = [[bb[C - 1:C, :] for bb in bt] for bt in bs]
        ks = [[gk_ref[r, ps] for ps in pair_cols] for r in rs]
        vs = [[gv_ref[r, cs].astype(bf16) for cs in cols] for r in rs]
        qe_pair = [[gq_ref[rs[t], pair_cols[g]] * jnp.exp(bs[t][g]) * (GLA_DK ** -0.5) for g in pairs]
                   for t in range(group)]
        qe = [[jnp.where(own_half[h % 2], qe_pair[t][h // 2], 0.0).astype(bf16) for h in heads]
              for t in range(group)]
        ke = [[(ks[t][g] * jnp.exp(-bs[t][g])).astype(bf16) for g in pairs] for t in range(group)]
        kd = [[(ks[t][g] * jnp.exp(b_last[t][g] - bs[t][g])).astype(bf16) for g in pairs] for t in range(group)]
        a = [[_dot_nt(qe[t][h], ke[t][h // 2]) for h in heads] for t in range(group)]
        u = [[_dot_tn(vs[t][h], kd[t][h // 2]) for h in heads] for t in range(group)]
        st = [st_ref[h] for h in heads]
        inter = []
        for t in range(group):
            inter.append([_dot_nt(qe[t][h], st[h].astype(bf16)) for h in heads])
            st = [st[h] * jnp.exp(b_last[t][h // 2]) + u[t][h] for h in heads]
        for h in heads:
            st_ref[h] = st[h]
        for t in range(group):
            for h in heads:
                o = _dot(jnp.where(causal, a[t][h], 0.0).astype(bf16), vs[t][h]) + inter[t][h]
                ms = jnp.sum(o * o, axis=-1, keepdims=True) * (1.0 / GLA_DV)
                y = o * lax.rsqrt(ms + EPS) * gn * _silu(gg_ref[rs[t], cols[h]])
                out_ref[rs[t], cols[h]] = y.astype(out_ref.dtype)
        return carry

    lax.fori_loop(0, n_chunks // group, chunks, 0)


def _gla(gq, gk, gv, gg, w2, b2, gn, tg):
    B, T, _ = gq.shape
    tok = lambda w: pl.BlockSpec((None, tg, w), lambda b, t: (b, t, 0))
    const = lambda s: pl.BlockSpec(s, lambda b, t: (0, 0))
    return pl.pallas_call(
        _gla_kernel,
        grid=(B, T // tg),
        in_specs=[tok(GLA_QK_W), tok(GLA_QK_W), tok(GLA_HP), tok(GLA_HP),
                  const((GLA_QK_W, GLA_QK_W)), const((1, GLA_QK_W)), const((1, LANE))],
        out_specs=tok(GLA_HP),
        out_shape=jax.ShapeDtypeStruct((B, T, GLA_HP), bf16),
        scratch_shapes=[pltpu.VMEM((GLA_HEADS, LANE, LANE), f32),
                        pltpu.VMEM((tg, GLA_QK_W), f32)],
        compiler_params=_params(),
        name="gla",
    )(gq, gk, gv, gg, w2, b2, gn)


def _pool_mixer(pu_ref, pg_ref, w_ref, sc_ref, halo_ref):
    tp = pu_ref.shape[0]
    H = POOL_HALO

    @pl.when(pl.program_id(1) == 0)
    def _():
        halo_ref[...] = jnp.zeros_like(halo_ref)

    u = pu_ref[...]
    ext = jnp.concatenate([halo_ref[...], u], axis=0)
    halo_ref[...] = u[tp - H:, :]
    s2 = ext + pltpu.roll(ext, 1, 0)
    s4 = s2 + pltpu.roll(s2, 2, 0)
    s8 = s4 + pltpu.roll(s4, 4, 0)
    s16 = s8 + pltpu.roll(s8, 8, 0)
    lane = lax.broadcasted_iota(i32, (tp, POOL_W), 1)
    grp = lane // POOL_GC
    sw = jnp.where(grp == 0, s2[H:], jnp.where(grp == 1, s4[H:], jnp.where(grp == 2, s8[H:], s16[H:])))
    win = jnp.where(grp == 0, POOL_WINDOWS[0],
                    jnp.where(grp == 1, POOL_WINDOWS[1],
                              jnp.where(grp == 2, POOL_WINDOWS[2], POOL_WINDOWS[3])))
    t = pl.program_id(1) * tp + lax.broadcasted_iota(i32, (tp, POOL_W), 0)
    cnt = jnp.minimum(t + 1, win).astype(f32)
    pooled = sw / cnt - u
    y = _dot(pooled.astype(bf16), w_ref[...]) * sc_ref[...]
    return (y * _silu(pg_ref[...])).astype(bf16)


def _mix_kernel(x_ref, yg_ref, yd_ref, pu_ref, pg_ref, wg_ref, wd_ref, wp_ref, wbd_ref, sc_ref, g_ref,
                *rest, project):
    halo_ref = rest[-1]
    yp = _pool_mixer(pu_ref, pg_ref, wbd_ref, sc_ref, halo_ref)
    x = x_ref[...] + _dot(yg_ref[...], wg_ref[...]) + _dot(yd_ref[...], wd_ref[...]) + _dot(yp, wp_ref[...])
    if project:
        wn_ref, wt_ref, xo_ref = rest[:3]
        xo_ref[...] = x
        _project(x, g_ref, wn_ref, wt_ref, rest[3:-1])
    else:
        rest[0][...] = x * lax.rsqrt(jnp.mean(x * x, axis=-1, keepdims=True) + EPS) * g_ref[...]


def _mix(x, yg, yd, pu, pg, wg, wd, wp, wbd, sc, g, wn=None, wt=None, *, tm):
    B, T, D = x.shape
    project = wn is not None
    tok = lambda w: pl.BlockSpec((None, tm, w), lambda b, t: (b, t, 0))
    const = lambda s: pl.BlockSpec(s, lambda b, t: (0, 0), pipeline_mode=pl.Buffered(1))
    in_specs = [tok(D), tok(GLA_HP), tok(DSA_W), tok(POOL_W), tok(POOL_W),
                const((GLA_HP, D)), const((DSA_W, D)), const((POOL_W, D)), const((POOL_W, POOL_W)),
                const((1, POOL_W)), const((1, D))]
    args = [x, yg, yd, pu, pg, wg, wd, wp, wbd, sc, g]
    out_shape, out_specs = [jax.ShapeDtypeStruct((B, T, D), f32)], [tok(D)]
    if project:
        in_specs += [const((D, NAT_COLS)), const((TR_ROWS, D))]
        args += [wn, wt]
        proj_shape, proj_specs = _proj_outs(B, T, tm)
        out_shape += proj_shape
        out_specs += proj_specs
    out = pl.pallas_call(
        functools.partial(_mix_kernel, project=project),
        grid=(B, T // tm),
        in_specs=in_specs,
        out_specs=out_specs,
        out_shape=out_shape,
        scratch_shapes=[pltpu.VMEM((POOL_HALO, POOL_W), f32)],
        compiler_params=_params(),
        name="mix_project" if project else "mix_final",
    )(*args)
    return (out[0], out[1:]) if project else out[0]


def _dsa_kernel(qT_ref, qiT_ref, wiT_ref, k_ref, kx_ref, vT_ref, dg_ref, brow_ref, bmax_ref, out_ref,
                keys_ref, planes_ref, active_ref, qm_ref, qim_ref, acc_ref, kmax_ref, bias_ref, *, topk):
    tq = qT_ref.shape[1]
    tk = tq
    qb = pl.program_id(1)
    n_blocks = qb + 1

    def rows(kb):
        return pl.ds(pl.multiple_of(kb * tk, tk), tk)

    @pl.when(qb == 0)
    def _():
        def body(kb, mx):
            kk = k_ref[rows(kb), :].astype(f32)
            n2 = jnp.sum(kk * kk, axis=1, keepdims=True)
            return jnp.maximum(mx, jnp.max(n2, axis=0, keepdims=True))
        mx = lax.fori_loop(0, k_ref.shape[0] // tk, body, jnp.zeros((1, 1), f32))
        kmax_ref[...] = jnp.broadcast_to(jnp.sqrt(mx), kmax_ref.shape)

    prow = lax.broadcasted_iota(i32, (LANE, tq), 0)
    for h in range(DSA_HEADS):
        pair = qT_ref[(h // 2) * LANE:(h // 2 + 1) * LANE, :]
        qm_ref[h // 2, :, (h % 2) * tq:(h % 2 + 1) * tq] = jnp.where(
            (prow // DSA_DH) == (h % 2), pair, jnp.zeros_like(pair))
    qi = qiT_ref[...]
    for h in range(IDX_HEADS):
        qim_ref[:, h * tq:(h + 1) * tq] = jnp.where((prow // IDX_DIM) == h, qi, jnp.zeros_like(qi))
    w = wiT_ref[...] * ((IDX_DIM ** -0.5) * (IDX_HEADS ** -0.5))

    def score_keys(r):
        d = _dot(kx_ref[r, :], qim_ref[...])
        s = jnp.zeros((d.shape[0], tq), f32)
        for h in range(IDX_HEADS):
            s = s + jnp.maximum(d[:, h * tq:(h + 1) * tq], 0.0) * w[h:h + 1, :]
        bits = lax.bitcast_convert_type(s, i32)
        return jnp.where(bits < 0, INT_MIN - bits, bits)

    def store_keys(kb, keys):
        keys_ref[rows(kb), :] = keys
        u = keys ^ INT_MIN
        words = [u[SUBLANE * k:SUBLANE * (k + 1), :] for k in range(32)]
        j, m = 16, 0x0000FFFF
        while j:
            for k in range(32):
                if k & j == 0:
                    t = (lax.shift_right_logical(words[k], jnp.int32(j)) ^ words[k + j]) & m
                    words[k + j] = words[k + j] ^ t
                    words[k] = words[k] ^ lax.shift_left(t, jnp.int32(j))
            j >>= 1
            m ^= m << j
        r8 = pl.ds(pl.multiple_of(kb * SUBLANE, SUBLANE), SUBLANE)
        for b in range(32):
            planes_ref[b, r8, :] = words[b]

    def score_body(i, c, span):
        keys = score_keys(pl.ds(pl.multiple_of(i * span, span), span))
        for part in range(span // tk):
            store_keys(i * (span // tk) + part, keys[part * tk:(part + 1) * tk, :])
        return c

    @pl.when((pl.program_id(0) == 0) & (qb == 0))
    def _():
        planes_ref[...] = jnp.zeros_like(planes_ref)
        for h in range(DSA_HEADS):
            for c in range(bias_ref.shape[1] // tk):
                x = jnp.broadcast_to(brow_ref[h:h + 1, :], (tk, brow_ref.shape[1]))
                x = pltpu.roll(x, c * tk, 1, stride=1, stride_axis=0)
                bias_ref[h, c * tk:(c + 1) * tk, :] = x[:, :tq]

    lax.fori_loop(0, qb // 4, functools.partial(score_body, span=4 * tk), 0)
    lax.fori_loop(2 * (qb // 4), qb // 2, functools.partial(score_body, span=2 * tk), 0)
    lax.fori_loop(2 * (qb // 2), qb, functools.partial(score_body, span=tk), 0)
    krow = lax.broadcasted_iota(i32, (tk, tq), 0)
    qcol = lax.broadcasted_iota(i32, (tk, tq), 1)
    store_keys(qb, jnp.where(krow <= qcol, score_keys(rows(qb)), INT_MIN))

    for extra in range(MASK_GROUP - 1):
        keys_ref[rows(n_blocks + extra), :] = jnp.full((tk, tq), INT_MIN, i32)

    n_rows = planes_ref.shape[1]
    prow8 = lax.broadcasted_iota(i32, (n_rows, tq), 0)
    active_ref[...] = jnp.where(prow8 < n_blocks * SUBLANE, -1, 0)

    half = n_rows // 2
    upper = n_blocks * SUBLANE > half

    def count_set(b, r0):
        ones = lax.population_count(active_ref[r0:r0 + half, :] & planes_ref[b, r0:r0 + half, :])
        return jnp.sum(jnp.sum(ones.reshape(half // SUBLANE, SUBLANE, tq), axis=0), axis=0, keepdims=True)

    def narrow(b, r0, flip):
        active_ref[r0:r0 + half, :] = active_ref[r0:r0 + half, :] & (planes_ref[b, r0:r0 + half, :] ^ flip)

    def radix_step(i, carry):
        n_gt, tau_u = carry
        b = 31 - i
        n_set = count_set(b, 0) + lax.cond(upper, lambda: count_set(b, half),
                                           lambda: jnp.zeros((1, tq), i32))
        take = n_gt + n_set >= topk
        flip = jnp.where(take, 0, -1)
        narrow(b, 0, flip)

        @pl.when(upper)
        def _():
            narrow(b, half, flip)

        return (jnp.where(take, n_gt, n_gt + n_set),
                tau_u | jnp.where(take, lax.shift_left(jnp.int32(1), b), 0))

    n_gt, tau_u = lax.fori_loop(0, 32, radix_step, (jnp.zeros((1, tq), i32), jnp.zeros((1, tq), i32)))
    tau = tau_u ^ INT_MIN
    need = jnp.where(tau == INT_MIN, 0, topk - n_gt).astype(f32)

    tri = (lax.broadcasted_iota(i32, (tk, tk), 1) <= lax.broadcasted_iota(i32, (tk, tk), 0)).astype(bf16)

    def mask_body(g, run):
        ks = [keys_ref[rows(MASK_GROUP * g + part), :] for part in range(MASK_GROUP)]
        eqs = [k == tau for k in ks]
        cnt = [_dot(tri, e.astype(bf16)) for e in eqs]
        for part in range(MASK_GROUP):
            pre = cnt[part] + run
            sel = (ks[part] > tau) | (eqs[part] & (pre <= need))
            keys_ref[rows(MASK_GROUP * g + part), :] = lax.bitcast_convert_type(
                jnp.where(sel, 0.0, NEG_BIG).astype(f32), i32)
            run = pre[tk - 1:tk, :]
        return run

    lax.fori_loop(0, (n_blocks + MASK_GROUP - 1) // MASK_GROUP, mask_body, jnp.zeros((1, tq), f32))

    def tile_ops(kb, span, near):
        r = pl.ds(pl.multiple_of(kb * tk, tk), span)
        ones = jnp.ones((ONES_ROWS, span), bf16)
        madd = lax.bitcast_convert_type(keys_ref[r, :], f32)
        dots = [_dot(k_ref[r, g * LANE:(g + 1) * LANE], qm_ref[g]) for g in range(DSA_HEADS // 2)]

        def logits(h):
            lg = dots[h // 2][:, (h % 2) * tq:(h % 2 + 1) * tq] + madd
            if near:
                j = kb - (qb - 2)
                lg = lg + bias_ref[h, pl.ds(pl.multiple_of(j * tk, tk), span), :]
            return lg

        def v1(h):
            return jnp.concatenate([vT_ref[h * DSA_DH:(h + 1) * DSA_DH, r], ones], axis=0)

        return logits, v1

    def attend_online(kb, ms, span, near):
        logits, v1 = tile_ops(kb, span, near)
        new_ms = []
        for h in range(DSA_HEADS):
            lg = logits(h)
            m_new = jnp.maximum(ms[h], jnp.max(lg, axis=0, keepdims=True))
            alpha = jnp.exp2(ms[h] - m_new)
            acc_ref[h] = alpha * acc_ref[h] + _dot(v1(h), jnp.exp2(lg - m_new).astype(bf16))
            new_ms.append(m_new)
        return tuple(new_ms)

    def attend_fixed(kb, shift, span, near):
        logits, v1 = tile_ops(kb, span, near)
        for h in range(DSA_HEADS):
            acc_ref[h] = acc_ref[h] + _dot(v1(h), jnp.exp2(logits(h) - shift[h]).astype(bf16))
        return shift

    def sweep(body, carry):
        n_far = jnp.maximum(qb - 1, 0)
        n_far_pairs = n_far // 2
        n_far_quads = n_far_pairs // 2
        has_prev = jnp.minimum(qb, 1)
        acc_ref[...] = jnp.zeros_like(acc_ref)
        carry = lax.fori_loop(0, n_far_quads, lambda i, c: body(4 * i, c, span=4 * tk, near=False), carry)
        carry = lax.fori_loop(2 * n_far_quads, n_far_pairs,
                              lambda i, c: body(2 * i, c, span=2 * tk, near=False), carry)
        carry = lax.fori_loop(2 * n_far_pairs, n_far, lambda i, c: body(i, c, span=tk, near=False), carry)
        carry = lax.fori_loop(0, has_prev, lambda i, c: body(qb - 1, c, span=2 * tk, near=True), carry)
        return lax.fori_loop(0, 1 - has_prev, lambda i, c: body(qb, c, span=tk, near=True), carry)

    q_norm = []
    for g in range(DSA_HEADS // 2):
        qf = qm_ref[g].astype(f32)
        n2 = jnp.sum(qf * qf, axis=0, keepdims=True)
        q_norm += [jnp.sqrt(n2[:, :tq]), jnp.sqrt(n2[:, tq:])]
    shift = tuple(q_norm[h] * (kmax_ref[0:1, 0:1] * SHIFT_MARGIN) + bmax_ref[h:h + 1, 0:1]
                  for h in range(DSA_HEADS))
    sweep(attend_fixed, shift)
    denom = acc_ref[0, DSA_DH:DSA_DH + 1, :]
    for h in range(1, DSA_HEADS):
        denom = jnp.minimum(denom, acc_ref[h, DSA_DH:DSA_DH + 1, :])
    accurate = jnp.min(denom) >= MIN_DENOM

    @pl.when(jnp.logical_not(accurate))
    def _():
        sweep(attend_online, tuple(jnp.full((1, tq), NEG_BIG, f32) for _ in range(DSA_HEADS)))

    oT = jnp.concatenate([acc_ref[h, :DSA_DH, :] / acc_ref[h, DSA_DH:DSA_DH + 1, :]
                          for h in range(DSA_HEADS)], axis=0)
    out_ref[...] = (oT.T * _silu(dg_ref[...])).astype(out_ref.dtype)


def _dsa(dqT, qiT, wiT, dk, kx, dvT, dg, bias, bmax, tq, topk):
    B, T, _ = dk.shape
    whole = lambda shape, imap: pl.BlockSpec(shape, imap, pipeline_mode=pl.Buffered(1))
    return pl.pallas_call(
        functools.partial(_dsa_kernel, topk=topk),
        grid=(B, T // tq),
        in_specs=[pl.BlockSpec((None, DSA_W, tq), lambda b, q: (b, 0, q)),
                  pl.BlockSpec((None, IDX_HEADS * IDX_DIM, tq), lambda b, q: (b, 0, q)),
                  pl.BlockSpec((None, SUBLANE, tq), lambda b, q: (b, 0, q)),
                  whole((None, T, DSA_W), lambda b, q: (b, 0, 0)),
                  whole((None, T, LANE), lambda b, q: (b, 0, 0)),
                  whole((None, DSA_W, T), lambda b, q: (b, 0, 0)),
                  pl.BlockSpec((None, tq, DSA_W), lambda b, q: (b, q, 0)),
                  pl.BlockSpec((SUBLANE, 4 * tq), lambda b, q: (0, 0)),
                  pl.BlockSpec((SUBLANE, LANE), lambda b, q: (0, 0))],
        out_specs=pl.BlockSpec((None, tq, DSA_W), lambda b, q: (b, q, 0)),
        out_shape=jax.ShapeDtypeStruct((B, T, DSA_W), bf16),
        scratch_shapes=[pltpu.VMEM((T + (MASK_GROUP - 1) * tq, tq), i32),
                        pltpu.VMEM((32, T // 32, tq), i32),
                        pltpu.VMEM((T // 32, tq), i32),
                        pltpu.VMEM((DSA_HEADS // 2, LANE, 2 * tq), bf16),
                        pltpu.VMEM((LANE, IDX_HEADS * tq), bf16),
                        pltpu.VMEM((DSA_HEADS, DSA_DH + ONES_ROWS, tq), f32),
                        pltpu.VMEM((SUBLANE, LANE), f32),
                        pltpu.VMEM((DSA_HEADS, 3 * tq, tq), f32)],
        compiler_params=_params(),
        name="dsa",
    )(dqT, qiT, wiT, dk, kx, dvT, dg, bias, bmax)


def _pad_heads(w, heads, d, slot=LANE):
    lead = w.shape[:-1]
    w = w.reshape(*lead, heads, d)
    w = jnp.pad(w, [(0, 0)] * len(lead) + [(0, 0), (0, slot - d)])
    return w.reshape(*lead, heads * slot)


def _split_cols(w):
    outs, off = [], 0
    for s in IN_SIZES:
        outs.append(w[..., off:off + s])
        off += s
    return outs


def _layer_weights(w_in, gate_w2, gate_b, gla_norm_g, pool_w, pool_scale, w_out):
    (gq, gk, gv, gz, gg, dq, dk, dv, dg, qi, ki, wi, pu, pg) = _split_cols(w_in)
    D = w_in.shape[0]
    gz_lanes = slice(GLA_DK, GLA_DK + GLA_GATE_RANK)
    gq_slab = _pad_heads(gq, GLA_HEADS, GLA_DK, GLA_QK_SLOT).at[:, gz_lanes].set(gz)
    wn = jnp.concatenate([
        gq_slab, _pad_heads(gk, GLA_HEADS, GLA_DK, GLA_QK_SLOT),
        _pad_heads(gv, GLA_HEADS, GLA_DV), _pad_heads(gg, GLA_HEADS, GLA_DV),
        dk, jnp.tile(ki, (1, IDX_HEADS)), dg, pu, pg], axis=1).astype(bf16)
    wt = jnp.concatenate([
        dq, dv, qi, jnp.pad(wi, ((0, 0), (0, SUBLANE - IDX_HEADS)))], axis=1).T.astype(bf16)
    w2 = jnp.zeros((GLA_QK_W, GLA_QK_W), f32).at[gz_lanes, :].set(
        _pad_heads(gate_w2, GLA_HEADS, GLA_DK, GLA_QK_SLOT)).astype(bf16)
    b2 = _pad_heads(gate_b[None, :], GLA_HEADS, GLA_DK, GLA_QK_SLOT)
    gn = jnp.pad(gla_norm_g[None, :], ((0, 0), (0, LANE - GLA_DV)))
    wbd = jax.scipy.linalg.block_diag(*[pool_w[g] for g in range(POOL_GROUPS)]).astype(bf16)
    sc = pool_scale[None, :]
    wg = _pad_heads(w_out[:GLA_W].T, GLA_HEADS, GLA_DV).T.astype(bf16)
    wd = w_out[GLA_W:GLA_W + DSA_W].astype(bf16)
    wp = w_out[GLA_W + DSA_W:].astype(bf16)
    return wn, wt, w2, b2, gn, wbd, sc, wg, wd, wp


def _t5_bucket_table(n):
    rel = np.arange(n)
    max_exact = REL_BUCKETS // 2
    relf = np.maximum(rel, 1).astype(np.float32)
    large = max_exact + (np.log(relf / np.float32(max_exact)) / np.float32(math.log(REL_MAX_DIST / max_exact))
                         * np.float32(REL_BUCKETS - max_exact)).astype(np.int32)
    large = np.minimum(large, REL_BUCKETS - 1)
    return np.where(rel < max_exact, rel, large)


def _near_bias(rel_bias, tq):
    bucket = _t5_bucket_table(3 * tq)
    assert np.all(bucket[tq + 1:] == REL_BUCKETS - 1)
    by_rel = (rel_bias[bucket] - rel_bias[REL_BUCKETS - 1][None, :]) * LOG2E
    n = 4 * tq
    u = np.arange(n)
    u = np.where(u < 2 * tq, u, u - n)
    brow = jnp.pad(by_rel[np.maximum(2 * tq + u, 0)].T, ((0, SUBLANE - DSA_HEADS), (0, 0)))
    bmax = jnp.max(by_rel, axis=0)
    bmax = jnp.broadcast_to(jnp.pad(bmax, (0, SUBLANE - DSA_HEADS))[:, None], (SUBLANE, LANE))
    return brow, bmax


def kernel(x, norm_g, w_in, gla_gate_w2, gla_gate_b, gla_norm_g, rel_bias, pool_w, pool_scale, w_out,
           final_norm_g):
    B, T, D = x.shape
    assert D == D_MODEL
    tm = min(512, T)
    tq = min(256, T)
    topk = min(TOPK_MAX, T // 4)
    assert T % tm == 0 and T % (2 * tq) == 0 and tq >= LANE and T % GLA_CHUNK == 0
    bias, bmax = _near_bias(rel_bias.astype(f32), tq)
    weights = [_layer_weights(w_in[l], gla_gate_w2[l], gla_gate_b[l], gla_norm_g[l], pool_w[l],
                              pool_scale[l], w_out[l]) for l in range(DEPTH)]
    proj = _in_proj(x, norm_g[0][None, :], weights[0][0], weights[0][1], tm)
    for l in range(DEPTH):
        _, _, w2, b2, gn, wbd, sc, wg, wd, wp = weights[l]
        (gq, gk, gv, gg, dk, kx, dg, pu, pg, dqT, dvT, qiT, wiT) = proj
        yg = _gla(gq, gk, gv, gg, w2, b2, gn, tm)
        yd = _dsa(dqT, qiT, wiT, dk, kx, dvT, dg, bias, bmax, tq, topk)
        if l + 1 < DEPTH:
            x, proj = _mix(x, yg, yd, pu, pg, wg, wd, wp, wbd, sc, norm_g[l + 1][None, :],
                           weights[l + 1][0], weights[l + 1][1], tm=tm)
        else:
            x = _mix(x, yg, yd, pu, pg, wg, wd, wp, wbd, sc, final_norm_g[None, :], tm=tm)
    return x
```

```python
import functools
import math

import numpy as np
import jax
import jax.numpy as jnp
from jax import lax
from jax.experimental import pallas as pl
from jax.experimental.pallas import tpu as pltpu

D_MODEL = 1024
DEPTH = 4
EPS = 1e-6
GLA_HEADS = 4
GLA_DK = 48
GLA_DV = 96
GLA_GATE_RANK = 16
GLA_GATE_NORM = 16.0
GLA_CHUNK = 64
DSA_HEADS = 6
DSA_DH = 64
IDX_HEADS = 4
IDX_DIM = 32
TOPK_MAX = 256
POOL_GROUPS = 4
POOL_GC = 64
POOL_WINDOWS = (2, 4, 8, 16)
REL_BUCKETS = 32
REL_MAX_DIST = 128

GLA_W = GLA_HEADS * GLA_DV
DSA_W = DSA_HEADS * DSA_DH
POOL_W = POOL_GROUPS * POOL_GC
IN_SIZES = (GLA_HEADS * GLA_DK, GLA_HEADS * GLA_DK, GLA_W, GLA_GATE_RANK, GLA_W,
            DSA_W, DSA_W, DSA_W, DSA_W, IDX_HEADS * IDX_DIM, IDX_DIM, IDX_HEADS,
            POOL_W, POOL_W)

LANE = 128
SUBLANE = 8
VMEM_LIMIT_BYTES = 56 * 1024 * 1024

GLA_HP = GLA_HEADS * LANE
GLA_QK_SLOT = LANE // 2
GLA_QK_W = GLA_HEADS * GLA_QK_SLOT
assert GLA_DK + GLA_GATE_RANK <= GLA_QK_SLOT
POOL_HALO = 16
ONES_ROWS = 16
LOG2E = math.log2(math.e)
DSA_Q_SCALE = (DSA_DH ** -0.5) * LOG2E
SHIFT_MARGIN = 1.01
MIN_DENOM = 2.0 ** -60
MASK_GROUP = 4
RADIX_PARTS = 4

_NAT = {}
_off = 0
for _name, _w in (("gq", GLA_QK_W), ("gk", GLA_QK_W), ("gv", GLA_HP), ("gg", GLA_HP),
                  ("dk", DSA_W), ("kx", LANE), ("dg", DSA_W), ("pu", POOL_W), ("pg", POOL_W)):
    _NAT[_name] = (_off, _off + _w)
    _off += _w
NAT_COLS = _off
_TR = {}
_off = 0
for _name, _w in (("dq", DSA_W), ("dv", DSA_W), ("qi", IDX_HEADS * IDX_DIM), ("wi", SUBLANE)):
    _TR[_name] = (_off, _off + _w)
    _off += _w
TR_ROWS = _off

INT_MIN = -2 ** 31
NEG_BIG = -1e30

f32 = jnp.float32
bf16 = jnp.bfloat16
i32 = jnp.int32


def _silu(x):
    return x * jax.nn.sigmoid(x)


def _dot(a, b):
    return jnp.dot(a, b, preferred_element_type=f32)


def _dot_nt(a, b):
    return lax.dot_general(a, b, (((1,), (1,)), ((), ())), preferred_element_type=f32)


def _dot_tn(a, b):
    return lax.dot_general(a, b, (((0,), (0,)), ((), ())), preferred_element_type=f32)


def _params():
    return pltpu.CompilerParams(dimension_semantics=("arbitrary", "arbitrary"),
                                vmem_limit_bytes=VMEM_LIMIT_BYTES)


_NAT_DTYPES = {"gq": f32, "gk": f32, "gv": f32, "gg": f32,
               "dk": bf16, "kx": bf16, "dg": f32, "pu": f32, "pg": f32}
_TR_DTYPES = {"dq": bf16, "dv": bf16, "qi": bf16, "wi": f32}
N_PROJ_OUTS = len(_NAT_DTYPES) + len(_TR_DTYPES)


def _project(x, g_ref, wn_ref, wt_ref, out_refs):
    h = x * lax.rsqrt(jnp.mean(x * x, axis=-1, keepdims=True) + EPS) * g_ref[...]
    hb = h.astype(bf16)
    nat_refs, tr_refs = out_refs[:len(_NAT_DTYPES)], out_refs[len(_NAT_DTYPES):]
    for name, ref in zip(_NAT_DTYPES, nat_refs):
        lo, hi = _NAT[name]
        ref[...] = _dot(hb, wn_ref[:, lo:hi]).astype(ref.dtype)
    for name, ref in zip(_TR_DTYPES, tr_refs):
        lo, hi = _TR[name]
        y = _dot_nt(wt_ref[lo:hi, :], hb)
        if name == "dq":
            y = y * DSA_Q_SCALE
        ref[...] = y.astype(ref.dtype)


def _proj_outs(B, T, tm):
    out_shape, out_specs = [], []
    for name, dt in _NAT_DTYPES.items():
        w = _NAT[name][1] - _NAT[name][0]
        out_shape.append(jax.ShapeDtypeStruct((B, T, w), dt))
        out_specs.append(pl.BlockSpec((None, tm, w), lambda b, t: (b, t, 0)))
    for name, dt in _TR_DTYPES.items():
        w = _TR[name][1] - _TR[name][0]
        out_shape.append(jax.ShapeDtypeStruct((B, w, T), dt))
        out_specs.append(pl.BlockSpec((None, w, tm), lambda b, t: (b, 0, t)))
    return out_shape, out_specs


def _in_proj_kernel(x_ref, g_ref, wn_ref, wt_ref, *out_refs):
    _project(x_ref[...], g_ref, wn_ref, wt_ref, out_refs)


def _in_proj(x, g, wn, wt, tm):
    B, T, D = x.shape
    out_shape, out_specs = _proj_outs(B, T, tm)
    return pl.pallas_call(
        _in_proj_kernel,
        grid=(B, T // tm),
        in_specs=[pl.BlockSpec((None, tm, D), lambda b, t: (b, t, 0)),
                  pl.BlockSpec((1, D), lambda b, t: (0, 0)),
                  pl.BlockSpec((D, NAT_COLS), lambda b, t: (0, 0)),
                  pl.BlockSpec((TR_ROWS, D), lambda b, t: (0, 0))],
        out_specs=out_specs,
        out_shape=out_shape,
        compiler_params=_params(),
        name="in_proj",
    )(x, g, wn, wt)


def _gla_kernel(gq_ref, gk_ref, gv_ref, gg_ref, w2_ref, b2_ref, gn_ref, out_ref,
                st_ref, glog_ref):
    C = GLA_CHUNK
    n_chunks = gq_ref.shape[0] // C

    @pl.when(pl.program_id(1) == 0)
    def _():
        st_ref[...] = jnp.zeros_like(st_ref)

    z = _dot(gq_ref[...].astype(bf16), w2_ref[...]) + b2_ref[...]
    b = (jnp.minimum(z, 0.0) - jnp.log1p(jnp.exp(-jnp.abs(z)))) / GLA_GATE_NORM
    pos = lax.broadcasted_iota(i32, b.shape, 0) & (C - 1)
    shift = 1
    while shift < C:
        b = b + jnp.where(pos >= shift, pltpu.roll(b, shift, 0), 0.0)
        shift *= 2
    glog_ref[...] = b

    causal = lax.broadcasted_iota(i32, (C, C), 1) <= lax.broadcasted_iota(i32, (C, C), 0)
    gn = gn_ref[...]
    heads = range(GLA_HEADS)
    pairs = range(GLA_HEADS // 2)
    cols = [slice(h * LANE, (h + 1) * LANE) for h in heads]
    pair_cols = [slice(g * LANE, (g + 1) * LANE) for g in pairs]
    lane_half = lax.broadcasted_iota(i32, (C, LANE), 1) // GLA_QK_SLOT
    own_half = [lane_half == 0, lane_half == 1]

    group = 2 if n_chunks % 2 == 0 else 1

    def chunks(c, carry):
        rs = [pl.ds(pl.multiple_of((c * group + t) * C, C), C) for t in range(group)]
        bs = [[glog_ref[r, ps] for ps in pair_cols] for r in rs]
        b_last = [[bb[C - 1:C, :] for bb in bt] for bt in bs]
        ks = [[gk_ref[r, ps] for ps in pair_cols] for r in rs]
        vs = [[gv_ref[r, cs].astype(bf16) for cs in cols] for r in rs]
        qe_pair = [[gq_ref[rs[t], pair_cols[g]] * jnp.exp(bs[t][g]) * (GLA_DK ** -0.5) for g in pairs]
                   for t in range(group)]
        qe = [[jnp.where(own_half[h % 2], qe_pair[t][h // 2], 0.0).astype(bf16) for h in heads]
              for t in range(group)]
        ke = [[(ks[t][g] * jnp.exp(-bs[t][g])).astype(bf16) for g in pairs] for t in range(group)]
        kd = [[(ks[t][g] * jnp.exp(b_last[t][g] - bs[t][g])).astype(bf16) for g in pairs] for t in range(group)]
        a = [[_dot_nt(qe[t][h], ke[t][h // 2]) for h in heads] for t in range(group)]
        u = [[_dot_tn(vs[t][h], kd[t][h // 2]) for h in heads] for t in range(group)]
        st = [st_ref[h] for h in heads]
        inter = []
        for t in range(group):
            inter.append([_dot_nt(qe[t][h], st[h].astype(bf16)) for h in heads])
            st = [st[h] * jnp.exp(b_last[t][h // 2]) + u[t][h] for h in heads]
        for h in heads:
            st_ref[h] = st[h]
        for t in range(group):
            for h in heads:
                o = _dot(jnp.where(causal, a[t][h], 0.0).astype(bf16), vs[t][h]) + inter[t][h]
                ms = jnp.sum(o * o, axis=-1, keepdims=True) * (1.0 / GLA_DV)
                y = o * lax.rsqrt(ms + EPS) * gn * _silu(gg_ref[rs[t], cols[h]])
                out_ref[rs[t], cols[h]] = y.astype(out_ref.dtype)
        return carry

    lax.fori_loop(0, n_chunks // group, chunks, 0)


def _gla(gq, gk, gv, gg, w2, b2, gn, tg):
    B, T, _ = gq.shape
    tok = lambda w: pl.BlockSpec((None, tg, w), lambda b, t: (b, t, 0))
    const = lambda s: pl.BlockSpec(s, lambda b, t: (0, 0))
    return pl.pallas_call(
        _gla_kernel,
        grid=(B, T // tg),
        in_specs=[tok(GLA_QK_W), tok(GLA_QK_W), tok(GLA_HP), tok(GLA_HP),
                  const((GLA_QK_W, GLA_QK_W)), const((1, GLA_QK_W)), const((1, LANE))],
        out_specs=tok(GLA_HP),
        out_shape=jax.ShapeDtypeStruct((B, T, GLA_HP), bf16),
        scratch_shapes=[pltpu.VMEM((GLA_HEADS, LANE, LANE), f32),
                        pltpu.VMEM((tg, GLA_QK_W), f32)],
        compiler_params=_params(),
        name="gla",
    )(gq, gk, gv, gg, w2, b2, gn)


def _pool_mixer(pu_ref, pg_ref, w_ref, sc_ref, halo_ref):
    tp = pu_ref.shape[0]
    H = POOL_HALO

    @pl.when(pl.program_id(1) == 0)
    def _():
        halo_ref[...] = jnp.zeros_like(halo_ref)

    u = pu_ref[...]
    ext = jnp.concatenate([halo_ref[...], u], axis=0)
    halo_ref[...] = u[tp - H:, :]
    s2 = ext + pltpu.roll(ext, 1, 0)
    s4 = s2 + pltpu.roll(s2, 2, 0)
    s8 = s4 + pltpu.roll(s4, 4, 0)
    s16 = s8 + pltpu.roll(s8, 8, 0)
    lane = lax.broadcasted_iota(i32, (tp, POOL_W), 1)
    grp = lane // POOL_GC
    sw = jnp.where(grp == 0, s2[H:], jnp.where(grp == 1, s4[H:], jnp.where(grp == 2, s8[H:], s16[H:])))
    win = jnp.where(grp == 0, POOL_WINDOWS[0],
                    jnp.where(grp == 1, POOL_WINDOWS[1],
                              jnp.where(grp == 2, POOL_WINDOWS[2], POOL_WINDOWS[3])))
    t = pl.program_id(1) * tp + lax.broadcasted_iota(i32, (tp, POOL_W), 0)
    cnt = jnp.minimum(t + 1, win).astype(f32)
    pooled = sw / cnt - u
    y = _dot(pooled.astype(bf16), w_ref[...]) * sc_ref[...]
    return (y * _silu(pg_ref[...])).astype(bf16)


def _mix_kernel(x_ref, yg_ref, yd_ref, pu_ref, pg_ref, wg_ref, wd_ref, wp_ref, wbd_ref, sc_ref, g_ref,
                *rest, project):
    halo_ref = rest[-1]
    yp = _pool_mixer(pu_ref, pg_ref, wbd_ref, sc_ref, halo_ref)
    x = x_ref[...] + _dot(yg_ref[...], wg_ref[...]) + _dot(yd_ref[...], wd_ref[...]) + _dot(yp, wp_ref[...])
    if project:
        wn_ref, wt_ref, xo_ref = rest[:3]
        xo_ref[...] = x
        _project(x, g_ref, wn_ref, wt_ref, rest[3:-1])
    else:
        rest[0][...] = x * lax.rsqrt(jnp.mean(x * x, axis=-1, keepdims=True) + EPS) * g_ref[...]


def _mix(x, yg, yd, pu, pg, wg, wd, wp, wbd, sc, g, wn=None, wt=None, *, tm):
    B, T, D = x.shape
    project = wn is not None
    tok = lambda w: pl.BlockSpec((None, tm, w), lambda b, t: (b, t, 0))
    const = lambda s: pl.BlockSpec(s, lambda b, t: (0, 0), pipeline_mode=pl.Buffered(1))
    in_specs = [tok(D), tok(GLA_HP), tok(DSA_W), tok(POOL_W), tok(POOL_W),
                const((GLA_HP, D)), const((DSA_W, D)), const((POOL_W, D)), const((POOL_W, POOL_W)),
                const((1, POOL_W)), const((1, D))]
    args = [x, yg, yd, pu, pg, wg, wd, wp, wbd, sc, g]
    out_shape, out_specs = [jax.ShapeDtypeStruct((B, T, D), f32)], [tok(D)]
    if project:
        in_specs += [const((D, NAT_COLS)), const((TR_ROWS, D))]
        args += [wn, wt]
        proj_shape, proj_specs = _proj_outs(B, T, tm)
        out_shape += proj_shape
        out_specs += proj_specs
    out = pl.pallas_call(
        functools.partial(_mix_kernel, project=project),
        grid=(B, T // tm),
        in_specs=in_specs,
        out_specs=out_specs,
        out_shape=out_shape,
        scratch_shapes=[pltpu.VMEM((POOL_HALO, POOL_W), f32)],
        compiler_params=_params(),
        name="mix_project" if project else "mix_final",
    )(*args)
    return (out[0], out[1:]) if project else out[0]


def _dsa_kernel(qT_ref, qiT_ref, wiT_ref, k_ref, kx_ref, vT_ref, dg_ref, brow_ref, bmax_ref, out_ref,
                keys_ref, planes_ref, active_ref, qm_ref, qim_ref, acc_ref, kmax_ref, bias_ref, *, topk):
    tq = qT_ref.shape[1]
    tk = tq
    qb = pl.program_id(1)
    n_blocks = qb + 1

    def rows(kb):
        return pl.ds(pl.multiple_of(kb * tk, tk), tk)

    @pl.when(qb == 0)
    def _():
        def body(kb, mx):
            kk = k_ref[rows(kb), :].astype(f32)
            n2 = jnp.sum(kk * kk, axis=1, keepdims=True)
            return jnp.maximum(mx, jnp.max(n2, axis=0, keepdims=True))
        mx = lax.fori_loop(0, k_ref.shape[0] // tk, body, jnp.zeros((1, 1), f32))
        kmax_ref[...] = jnp.broadcast_to(jnp.sqrt(mx), kmax_ref.shape)

    prow = lax.broadcasted_iota(i32, (LANE, tq), 0)
    for h in range(DSA_HEADS):
        pair = qT_ref[(h // 2) * LANE:(h // 2 + 1) * LANE, :]
        qm_ref[h // 2, :, (h % 2) * tq:(h % 2 + 1) * tq] = jnp.where(
            (prow // DSA_DH) == (h % 2), pair, jnp.zeros_like(pair))
    qi = qiT_ref[...]
    for h in range(IDX_HEADS):
        qim_ref[:, h * tq:(h + 1) * tq] = jnp.where((prow // IDX_DIM) == h, qi, jnp.zeros_like(qi))
    w = wiT_ref[...] * ((IDX_DIM ** -0.5) * (IDX_HEADS ** -0.5))

    def score_keys(r):
        d = _dot(kx_ref[r, :], qim_ref[...])
        s = jnp.zeros((d.shape[0], tq), f32)
        for h in range(IDX_HEADS):
            s = s + jnp.maximum(d[:, h * tq:(h + 1) * tq], 0.0) * w[h:h + 1, :]
        bits = lax.bitcast_convert_type(s, i32)
        return jnp.where(bits < 0, INT_MIN - bits, bits)

    def store_keys(kb, keys):
        keys_ref[rows(kb), :] = keys
        u = keys ^ INT_MIN
        words = [u[SUBLANE * k:SUBLANE * (k + 1), :] for k in range(32)]
        j, m = 16, 0x0000FFFF
        while j:
            for k in range(32):
                if k & j == 0:
                    t = (lax.shift_right_logical(words[k], jnp.int32(j)) ^ words[k + j]) & m
                    words[k + j] = words[k + j] ^ t
                    words[k] = words[k] ^ lax.shift_left(t, jnp.int32(j))
            j >>= 1
            m ^= m << j
        r8 = pl.ds(pl.multiple_of(kb * SUBLANE, SUBLANE), SUBLANE)
        for b in range(32):
            planes_ref[b, r8, :] = words[b]

    def score_body(i, c, span):
        keys = score_keys(pl.ds(pl.multiple_of(i * span, span), span))
        for part in range(span // tk):
            store_keys(i * (span // tk) + part, keys[part * tk:(part + 1) * tk, :])
        return c

    @pl.when((pl.program_id(0) == 0) & (qb == 0))
    def _():
        planes_ref[...] = jnp.zeros_like(planes_ref)
        for h in range(DSA_HEADS):
            for c in range(bias_ref.shape[1] // tk):
                x = jnp.broadcast_to(brow_ref[h:h + 1, :], (tk, brow_ref.shape[1]))
                x = pltpu.roll(x, c * tk, 1, stride=1, stride_axis=0)
                bias_ref[h, c * tk:(c + 1) * tk, :] = x[:, :tq]

    lax.fori_loop(0, qb // 4, functools.partial(score_body, span=4 * tk), 0)
    lax.fori_loop(2 * (qb // 4), qb // 2, functools.partial(score_body, span=2 * tk), 0)
    lax.fori_loop(2 * (qb // 2), qb, functools.partial(score_body, span=tk), 0)
    krow = lax.broadcasted_iota(i32, (tk, tq), 0)
    qcol = lax.broadcasted_iota(i32, (tk, tq), 1)
    store_keys(qb, jnp.where(krow <= qcol, score_keys(rows(qb)), INT_MIN))

    for extra in range(MASK_GROUP - 1):
        keys_ref[rows(n_blocks + extra), :] = jnp.full((tk, tq), INT_MIN, i32)

    n_rows = planes_ref.shape[1]
    prow8 = lax.broadcasted_iota(i32, (n_rows, tq), 0)
    active_ref[...] = jnp.where(prow8 < n_blocks * SUBLANE, -1, 0)

    part = n_rows // RADIX_PARTS
    reached = [n_blocks * SUBLANE > p * part for p in range(RADIX_PARTS)]

    def count_set(b, p):
        r = slice(p * part, (p + 1) * part)
        ones = lax.population_count(active_ref[r, :] & planes_ref[b, r, :])
        return jnp.sum(jnp.sum(ones.reshape(part // SUBLANE, SUBLANE, tq), axis=0), axis=0, keepdims=True)

    def narrow(b, p, flip):
        r = slice(p * part, (p + 1) * part)
        active_ref[r, :] = active_ref[r, :] & (planes_ref[b, r, :] ^ flip)

    def radix_step(i, carry):
        n_gt, tau_u = carry
        b = 31 - i
        n_set = count_set(b, 0)
        for p in range(1, RADIX_PARTS):
            n_set = n_set + lax.cond(reached[p], functools.partial(count_set, b, p),
                                     lambda: jnp.zeros((1, tq), i32))
        take = n_gt + n_set >= topk
        flip = jnp.where(take, 0, -1)
        narrow(b, 0, flip)
        for p in range(1, RADIX_PARTS):
            pl.when(reached[p])(functools.partial(narrow, b, p, flip))

        return (jnp.where(take, n_gt, n_gt + n_set),
                tau_u | jnp.where(take, lax.shift_left(jnp.int32(1), b), 0))

    n_gt, tau_u = lax.fori_loop(0, 32, radix_step, (jnp.zeros((1, tq), i32), jnp.zeros((1, tq), i32)))
    tau = tau_u ^ INT_MIN
    need = jnp.where(tau == INT_MIN, 0, topk - n_gt).astype(f32)

    tri = (lax.broadcasted_iota(i32, (tk, tk), 1) <= lax.broadcasted_iota(i32, (tk, tk), 0)).astype(bf16)

    def mask_body(g, run):
        ks = [keys_ref[rows(MASK_GROUP * g + part), :] for part in range(MASK_GROUP)]
        eqs = [k == tau for k in ks]
        cnt = [_dot(tri, e.astype(bf16)) for e in eqs]
        for part in range(MASK_GROUP):
            pre = cnt[part] + run
            sel = (ks[part] > tau) | (eqs[part] & (pre <= need))
            keys_ref[rows(MASK_GROUP * g + part), :] = lax.bitcast_convert_type(
                jnp.where(sel, 0.0, NEG_BIG).astype(f32), i32)
            run = pre[tk - 1:tk, :]
        return run

    lax.fori_loop(0, (n_blocks + MASK_GROUP - 1) // MASK_GROUP, mask_body, jnp.zeros((1, tq), f32))

    def tile_ops(kb, span, near):
        r = pl.ds(pl.multiple_of(kb * tk, tk), span)
        ones = jnp.ones((ONES_ROWS, span), bf16)
        madd = lax.bitcast_convert_type(keys_ref[r, :], f32)
        dots = [_dot(k_ref[r, g * LANE:(g + 1) * LANE], qm_ref[g]) for g in range(DSA_HEADS // 2)]

        def logits(h):
            lg = dots[h // 2][:, (h % 2) * tq:(h % 2 + 1) * tq] + madd
            if near:
                j = kb - (qb - 2)
                lg = lg + bias_ref[h, pl.ds(pl.multiple_of(j * tk, tk), span), :]
            return lg

        def v1(h):
            return jnp.concatenate([vT_ref[h * DSA_DH:(h + 1) * DSA_DH, r], ones], axis=0)

        return logits, v1

    def attend_online(kb, ms, span, near):
        logits, v1 = tile_ops(kb, span, near)
        new_ms = []
        for h in range(DSA_HEADS):
            lg = logits(h)
            m_new = jnp.maximum(ms[h], jnp.max(lg, axis=0, keepdims=True))
            alpha = jnp.exp2(ms[h] - m_new)
            acc_ref[h] = alpha * acc_ref[h] + _dot(v1(h), jnp.exp2(lg - m_new).astype(bf16))
            new_ms.append(m_new)
        return tuple(new_ms)

    def attend_fixed(kb, shift, span, near):
        logits, v1 = tile_ops(kb, span, near)
        for h in range(DSA_HEADS):
            acc_ref[h] = acc_ref[h] + _dot(v1(h), jnp.exp2(logits(h) - shift[h]).astype(bf16))
        return shift

    def sweep(body, carry):
        n_far = jnp.maximum(qb - 1, 0)
        n_far_pairs = n_far // 2
        n_far_quads = n_far_pairs // 2
        has_prev = jnp.minimum(qb, 1)
        acc_ref[...] = jnp.zeros_like(acc_ref)
        carry = lax.fori_loop(0, n_far_quads, lambda i, c: body(4 * i, c, span=4 * tk, near=False), carry)
        carry = lax.fori_loop(2 * n_far_quads, n_far_pairs,
                              lambda i, c: body(2 * i, c, span=2 * tk, near=False), carry)
        carry = lax.fori_loop(2 * n_far_pairs, n_far, lambda i, c: body(i, c, span=tk, near=False), carry)
        carry = lax.fori_loop(0, has_prev, lambda i, c: body(qb - 1, c, span=2 * tk, near=True), carry)
        return lax.fori_loop(0, 1 - has_prev, lambda i, c: body(qb, c, span=tk, near=True), carry)

    q_norm = []
    for g in range(DSA_HEADS // 2):
        qf = qm_ref[g].astype(f32)
        n2 = jnp.sum(qf * qf, axis=0, keepdims=True)
        q_norm += [jnp.sqrt(n2[:, :tq]), jnp.sqrt(n2[:, tq:])]
    shift = tuple(q_norm[h] * (kmax_ref[0:1, 0:1] * SHIFT_MARGIN) + bmax_ref[h:h + 1, 0:1]
                  for h in range(DSA_HEADS))
    sweep(attend_fixed, shift)
    denom = acc_ref[0, DSA_DH:DSA_DH + 1, :]
    for h in range(1, DSA_HEADS):
        denom = jnp.minimum(denom, acc_ref[h, DSA_DH:DSA_DH + 1, :])
    accurate = jnp.min(denom) >= MIN_DENOM

    @pl.when(jnp.logical_not(accurate))
    def _():
        sweep(attend_online, tuple(jnp.full((1, tq), NEG_BIG, f32) for _ in range(DSA_HEADS)))

    oT = jnp.concatenate([acc_ref[h, :DSA_DH, :] / acc_ref[h, DSA_DH:DSA_DH + 1, :]
                          for h in range(DSA_HEADS)], axis=0)
    out_ref[...] = (oT.T * _silu(dg_ref[...])).astype(out_ref.dtype)


def _dsa(dqT, qiT, wiT, dk, kx, dvT, dg, bias, bmax, tq, topk):
    B, T, _ = dk.shape
    whole = lambda shape, imap: pl.BlockSpec(shape, imap, pipeline_mode=pl.Buffered(1))
    return pl.pallas_call(
        functools.partial(_dsa_kernel, topk=topk),
        grid=(B, T // tq),
        in_specs=[pl.BlockSpec((None, DSA_W, tq), lambda b, q: (b, 0, q)),
                  pl.BlockSpec((None, IDX_HEADS * IDX_DIM, tq), lambda b, q: (b, 0, q)),
                  pl.BlockSpec((None, SUBLANE, tq), lambda b, q: (b, 0, q)),
                  whole((None, T, DSA_W), lambda b, q: (b, 0, 0)),
                  whole((None, T, LANE), lambda b, q: (b, 0, 0)),
                  whole((None, DSA_W, T), lambda b, q: (b, 0, 0)),
                  pl.BlockSpec((None, tq, DSA_W), lambda b, q: (b, q, 0)),
                  pl.BlockSpec((SUBLANE, 4 * tq), lambda b, q: (0, 0)),
                  pl.BlockSpec((SUBLANE, LANE), lambda b, q: (0, 0))],
        out_specs=pl.BlockSpec((None, tq, DSA_W), lambda b, q: (b, q, 0)),
        out_shape=jax.ShapeDtypeStruct((B, T, DSA_W), bf16),
        scratch_shapes=[pltpu.VMEM((T + (MASK_GROUP - 1) * tq, tq), i32),
                        pltpu.VMEM((32, T // 32, tq), i32),
                        pltpu.VMEM((T // 32, tq), i32),
                        pltpu.VMEM((DSA_HEADS // 2, LANE, 2 * tq), bf16),
                        pltpu.VMEM((LANE, IDX_HEADS * tq), bf16),
                        pltpu.VMEM((DSA_HEADS, DSA_DH + ONES_ROWS, tq), f32),
                        pltpu.VMEM((SUBLANE, LANE), f32),
                        pltpu.VMEM((DSA_HEADS, 3 * tq, tq), f32)],
        compiler_params=_params(),
        name="dsa",
    )(dqT, qiT, wiT, dk, kx, dvT, dg, bias, bmax)


def _pad_heads(w, heads, d, slot=LANE):
    lead = w.shape[:-1]
    w = w.reshape(*lead, heads, d)
    w = jnp.pad(w, [(0, 0)] * len(lead) + [(0, 0), (0, slot - d)])
    return w.reshape(*lead, heads * slot)


def _split_cols(w):
    outs, off = [], 0
    for s in IN_SIZES:
        outs.append(w[..., off:off + s])
        off += s
    return outs


def _layer_weights(w_in, gate_w2, gate_b, gla_norm_g, pool_w, pool_scale, w_out):
    (gq, gk, gv, gz, gg, dq, dk, dv, dg, qi, ki, wi, pu, pg) = _split_cols(w_in)
    D = w_in.shape[0]
    gz_lanes = slice(GLA_DK, GLA_DK + GLA_GATE_RANK)
    gq_slab = _pad_heads(gq, GLA_HEADS, GLA_DK, GLA_QK_SLOT).at[:, gz_lanes].set(gz)
    wn = jnp.concatenate([
        gq_slab, _pad_heads(gk, GLA_HEADS, GLA_DK, GLA_QK_SLOT),
        _pad_heads(gv, GLA_HEADS, GLA_DV), _pad_heads(gg, GLA_HEADS, GLA_DV),
        dk, jnp.tile(ki, (1, IDX_HEADS)), dg, pu, pg], axis=1).astype(bf16)
    wt = jnp.concatenate([
        dq, dv, qi, jnp.pad(wi, ((0, 0), (0, SUBLANE - IDX_HEADS)))], axis=1).T.astype(bf16)
    w2 = jnp.zeros((GLA_QK_W, GLA_QK_W), f32).at[gz_lanes, :].set(
        _pad_heads(gate_w2, GLA_HEADS, GLA_DK, GLA_QK_SLOT)).astype(bf16)
    b2 = _pad_heads(gate_b[None, :], GLA_HEADS, GLA_DK, GLA_QK_SLOT)
    gn = jnp.pad(gla_norm_g[None, :], ((0, 0), (0, LANE - GLA_DV)))
    wbd = jax.scipy.linalg.block_diag(*[pool_w[g] for g in range(POOL_GROUPS)]).astype(bf16)
    sc = pool_scale[None, :]
    wg = _pad_heads(w_out[:GLA_W].T, GLA_HEADS, GLA_DV).T.astype(bf16)
    wd = w_out[GLA_W:GLA_W + DSA_W].astype(bf16)
    wp = w_out[GLA_W + DSA_W:].astype(bf16)
    return wn, wt, w2, b2, gn, wbd, sc, wg, wd, wp


def _t5_bucket_table(n):
    rel = np.arange(n)
    max_exact = REL_BUCKETS // 2
    relf = np.maximum(rel, 1).astype(np.float32)
    large = max_exact + (np.log(relf / np.float32(max_exact)) / np.float32(math.log(REL_MAX_DIST / max_exact))
                         * np.float32(REL_BUCKETS - max_exact)).astype(np.int32)
    large = np.minimum(large, REL_BUCKETS - 1)
    return np.where(rel < max_exact, rel, large)


def _near_bias(rel_bias, tq):
    bucket = _t5_bucket_table(3 * tq)
    assert np.all(bucket[tq + 1:] == REL_BUCKETS - 1)
    by_rel = (rel_bias[bucket] - rel_bias[REL_BUCKETS - 1][None, :]) * LOG2E
    n = 4 * tq
    u = np.arange(n)
    u = np.where(u < 2 * tq, u, u - n)
    brow = jnp.pad(by_rel[np.maximum(2 * tq + u, 0)].T, ((0, SUBLANE - DSA_HEADS), (0, 0)))
    bmax = jnp.max(by_rel, axis=0)
    bmax = jnp.broadcast_to(jnp.pad(bmax, (0, SUBLANE - DSA_HEADS))[:, None], (SUBLANE, LANE))
    return brow, bmax


def kernel(x, norm_g, w_in, gla_gate_w2, gla_gate_b, gla_norm_g, rel_bias, pool_w, pool_scale, w_out,
           final_norm_g):
    B, T, D = x.shape
    assert D == D_MODEL
    tm = min(512, T)
    tq = min(256, T)
    topk = min(TOPK_MAX, T // 4)
    assert T % tm == 0 and T % (2 * tq) == 0 and tq >= LANE and T % GLA_CHUNK == 0
    bias, bmax = _near_bias(rel_bias.astype(f32), tq)
    weights = [_layer_weights(w_in[l], gla_gate_w2[l], gla_gate_b[l], gla_norm_g[l], pool_w[l],
                              pool_scale[l], w_out[l]) for l in range(DEPTH)]
    proj = _in_proj(x, norm_g[0][None, :], weights[0][0], weights[0][1], tm)
    for l in range(DEPTH):
        _, _, w2, b2, gn, wbd, sc, wg, wd, wp = weights[l]
        (gq, gk, gv, gg, dk, kx, dg, pu, pg, dqT, dvT, qiT, wiT) = proj
        yg = _gla(gq, gk, gv, gg, w2, b2, gn, tm)
        yd = _dsa(dqT, qiT, wiT, dk, kx, dvT, dg, bias, bmax, tq, topk)
        if l + 1 < DEPTH:
            x, proj = _mix(x, yg, yd, pu, pg, wg, wd, wp, wbd, sc, norm_g[l + 1][None, :],
                           weights[l + 1][0], weights[l + 1][1], tm=tm)
        else:
            x = _mix(x, yg, yd, pu, pg, wg, wd, wp, wbd, sc, final_norm_g[None, :], tm=tm)
    return x
```

```python
import functools
import math

import numpy as np
import jax
import jax.numpy as jnp
from jax import lax
from jax.experimental import pallas as pl
from jax.experimental.pallas import tpu as pltpu

D_MODEL = 1024
DEPTH = 4
EPS = 1e-6
GLA_HEADS = 4
GLA_DK = 48
GLA_DV = 96
GLA_GATE_RANK = 16
GLA_GATE_NORM = 16.0
GLA_CHUNK = 64
DSA_HEADS = 6
DSA_DH = 64
IDX_HEADS = 4
IDX_DIM = 32
TOPK_MAX = 256
POOL_GROUPS = 4
POOL_GC = 64
POOL_WINDOWS = (2, 4, 8, 16)
REL_BUCKETS = 32
REL_MAX_DIST = 128

GLA_W = GLA_HEADS * GLA_DV
DSA_W = DSA_HEADS * DSA_DH
POOL_W = POOL_GROUPS * POOL_GC
IN_SIZES = (GLA_HEADS * GLA_DK, GLA_HEADS * GLA_DK, GLA_W, GLA_GATE_RANK, GLA_W,
            DSA_W, DSA_W, DSA_W, DSA_W, IDX_HEADS * IDX_DIM, IDX_DIM, IDX_HEADS,
            POOL_W, POOL_W)

LANE = 128
SUBLANE = 8
VMEM_LIMIT_BYTES = 56 * 1024 * 1024

GLA_HP = GLA_HEADS * LANE
GLA_QK_SLOT = LANE // 2
GLA_QK_W = GLA_HEADS * GLA_QK_SLOT
assert GLA_DK + GLA_GATE_RANK <= GLA_QK_SLOT
POOL_HALO = 16
ONES_ROWS = 16
LOG2E = math.log2(math.e)
DSA_Q_SCALE = (DSA_DH ** -0.5) * LOG2E
SHIFT_MARGIN = 1.01
MIN_DENOM = 2.0 ** -60
MASK_GROUP = 4

_NAT = {}
_off = 0
for _name, _w in (("gq", GLA_QK_W), ("gk", GLA_QK_W), ("gv", GLA_HP), ("gg", GLA_HP),
                  ("dk", DSA_W), ("kx", LANE), ("dg", DSA_W), ("pu", POOL_W), ("pg", POOL_W)):
    _NAT[_name] = (_off, _off + _w)
    _off += _w
NAT_COLS = _off
_TR = {}
_off = 0
for _name, _w in (("dq", DSA_W), ("dv", DSA_W), ("qi", IDX_HEADS * IDX_DIM), ("wi", SUBLANE)):
    _TR[_name] = (_off, _off + _w)
    _off += _w
TR_ROWS = _off

INT_MIN = -2 ** 31
NEG_BIG = -1e30

f32 = jnp.float32
bf16 = jnp.bfloat16
i32 = jnp.int32


def _silu(x):
    return x * jax.nn.sigmoid(x)


def _dot(a, b):
    return jnp.dot(a, b, preferred_element_type=f32)


def _dot_nt(a, b):
    return lax.dot_general(a, b, (((1,), (1,)), ((), ())), preferred_element_type=f32)


def _dot_tn(a, b):
    return lax.dot_general(a, b, (((0,), (0,)), ((), ())), preferred_element_type=f32)


def _params():
    return pltpu.CompilerParams(dimension_semantics=("arbitrary", "arbitrary"),
                                vmem_limit_bytes=VMEM_LIMIT_BYTES)


_NAT_DTYPES = {"gq": f32, "gk": f32, "gv": f32, "gg": f32,
               "dk": bf16, "kx": bf16, "dg": f32, "pu": f32, "pg": f32}
_TR_DTYPES = {"dq": bf16, "dv": bf16, "qi": bf16, "wi": f32}
N_PROJ_OUTS = len(_NAT_DTYPES) + len(_TR_DTYPES)


def _project(x, g_ref, wn_ref, wt_ref, out_refs):
    h = x * lax.rsqrt(jnp.mean(x * x, axis=-1, keepdims=True) + EPS) * g_ref[...]
    hb = h.astype(bf16)
    nat_refs, tr_refs = out_refs[:len(_NAT_DTYPES)], out_refs[len(_NAT_DTYPES):]
    for name, ref in zip(_NAT_DTYPES, nat_refs):
        lo, hi = _NAT[name]
        ref[...] = _dot(hb, wn_ref[:, lo:hi]).astype(ref.dtype)
    for name, ref in zip(_TR_DTYPES, tr_refs):
        lo, hi = _TR[name]
        y = _dot_nt(wt_ref[lo:hi, :], hb)
        if name == "dq":
            y = y * DSA_Q_SCALE
        ref[...] = y.astype(ref.dtype)


def _proj_outs(B, T, tm):
    out_shape, out_specs = [], []
    for name, dt in _NAT_DTYPES.items():
        w = _NAT[name][1] - _NAT[name][0]
        out_shape.append(jax.ShapeDtypeStruct((B, T, w), dt))
        out_specs.append(pl.BlockSpec((None, tm, w), lambda b, t: (b, t, 0)))
    for name, dt in _TR_DTYPES.items():
        w = _TR[name][1] - _TR[name][0]
        out_shape.append(jax.ShapeDtypeStruct((B, w, T), dt))
        out_specs.append(pl.BlockSpec((None, w, tm), lambda b, t: (b, 0, t)))
    return out_shape, out_specs


def _in_proj_kernel(x_ref, g_ref, wn_ref, wt_ref, *out_refs):
    _project(x_ref[...], g_ref, wn_ref, wt_ref, out_refs)


def _in_proj(x, g, wn, wt, tm):
    B, T, D = x.shape
    out_shape, out_specs = _proj_outs(B, T, tm)
    return pl.pallas_call(
        _in_proj_kernel,
        grid=(B, T // tm),
        in_specs=[pl.BlockSpec((None, tm, D), lambda b, t: (b, t, 0)),
                  pl.BlockSpec((1, D), lambda b, t: (0, 0)),
                  pl.BlockSpec((D, NAT_COLS), lambda b, t: (0, 0)),
                  pl.BlockSpec((TR_ROWS, D), lambda b, t: (0, 0))],
        out_specs=out_specs,
        out_shape=out_shape,
        compiler_params=_params(),
        name="in_proj",
    )(x, g, wn, wt)


def _gla_kernel(gq_ref, gk_ref, gv_ref, gg_ref, w2_ref, b2_ref, gn_ref, out_ref,
                st_ref, glog_ref):
    C = GLA_CHUNK
    n_chunks = gq_ref.shape[0] // C

    @pl.when(pl.program_id(1) == 0)
    def _():
        st_ref[...] = jnp.zeros_like(st_ref)

    z = _dot(gq_ref[...].astype(bf16), w2_ref[...]) + b2_ref[...]
    b = (jnp.minimum(z, 0.0) - jnp.log1p(jnp.exp(-jnp.abs(z)))) / GLA_GATE_NORM
    pos = lax.broadcasted_iota(i32, b.shape, 0) & (C - 1)
    shift = 1
    while shift < C:
        b = b + jnp.where(pos >= shift, pltpu.roll(b, shift, 0), 0.0)
        shift *= 2
    glog_ref[...] = b

    causal = lax.broadcasted_iota(i32, (C, C), 1) <= lax.broadcasted_iota(i32, (C, C), 0)
    gn = gn_ref[...]
    heads = range(GLA_HEADS)
    pairs = range(GLA_HEADS // 2)
    cols = [slice(h * LANE, (h + 1) * LANE) for h in heads]
    pair_cols = [slice(g * LANE, (g + 1) * LANE) for g in pairs]
    lane_half = lax.broadcasted_iota(i32, (C, LANE), 1) // GLA_QK_SLOT
    own_half = [lane_half == 0, lane_half == 1]

    group = max(g for g in (4, 2, 1) if n_chunks % g == 0)

    def chunks(c, carry):
        rs = [pl.ds(pl.multiple_of((c * group + t) * C, C), C) for t in range(group)]
        bs = [[glog_ref[r, ps] for ps in pair_cols] for r in rs]
        b_last = [[bb[C - 1:C, :] for bb in bt] for bt in bs]
        ks = [[gk_ref[r, ps] for ps in pair_cols] for r in rs]
        vs = [[gv_ref[r, cs].astype(bf16) for cs in cols] for r in rs]
        qe_pair = [[gq_ref[rs[t], pair_cols[g]] * jnp.exp(bs[t][g]) * (GLA_DK ** -0.5) for g in pairs]
                   for t in range(group)]
        qe = [[jnp.where(own_half[h % 2], qe_pair[t][h // 2], 0.0).astype(bf16) for h in heads]
              for t in range(group)]
        ke = [[(ks[t][g] * jnp.exp(-bs[t][g])).astype(bf16) for g in pairs] for t in range(group)]
        kd = [[(ks[t][g] * jnp.exp(b_last[t][g] - bs[t][g])).astype(bf16) for g in pairs] for t in range(group)]
        a = [[_dot_nt(qe[t][h], ke[t][h // 2]) for h in heads] for t in range(group)]
        u = [[_dot_tn(vs[t][h], kd[t][h // 2]) for h in heads] for t in range(group)]
        st = [st_ref[h] for h in heads]
        inter = []
        for t in range(group):
            inter.append([_dot_nt(qe[t][h], st[h].astype(bf16)) for h in heads])
            st = [st[h] * jnp.exp(b_last[t][h // 2]) + u[t][h] for h in heads]
        for h in heads:
            st_ref[h] = st[h]
        for t in range(group):
            for h in heads:
                o = _dot(jnp.where(causal, a[t][h], 0.0).astype(bf16), vs[t][h]) + inter[t][h]
                ms = jnp.sum(o * o, axis=-1, keepdims=True) * (1.0 / GLA_DV)
                y = o * lax.rsqrt(ms + EPS) * gn * _silu(gg_ref[rs[t], cols[h]])
                out_ref[rs[t], cols[h]] = y.astype(out_ref.dtype)
        return carry

    lax.fori_loop(0, n_chunks // group, chunks, 0)


def _gla(gq, gk, gv, gg, w2, b2, gn, tg):
    B, T, _ = gq.shape
    tok = lambda w: pl.BlockSpec((None, tg, w), lambda b, t: (b, t, 0))
    const = lambda s: pl.BlockSpec(s, lambda b, t: (0, 0))
    return pl.pallas_call(
        _gla_kernel,
        grid=(B, T // tg),
        in_specs=[tok(GLA_QK_W), tok(GLA_QK_W), tok(GLA_HP), tok(GLA_HP),
                  const((GLA_QK_W, GLA_QK_W)), const((1, GLA_QK_W)), const((1, LANE))],
        out_specs=tok(GLA_HP),
        out_shape=jax.ShapeDtypeStruct((B, T, GLA_HP), bf16),
        scratch_shapes=[pltpu.VMEM((GLA_HEADS, LANE, LANE), f32),
                        pltpu.VMEM((tg, GLA_QK_W), f32)],
        compiler_params=_params(),
        name="gla",
    )(gq, gk, gv, gg, w2, b2, gn)


def _pool_mixer(pu_ref, pg_ref, w_ref, sc_ref, halo_ref):
    tp = pu_ref.shape[0]
    H = POOL_HALO

    @pl.when(pl.program_id(1) == 0)
    def _():
        halo_ref[...] = jnp.zeros_like(halo_ref)

    u = pu_ref[...]
    ext = jnp.concatenate([halo_ref[...], u], axis=0)
    halo_ref[...] = u[tp - H:, :]
    s2 = ext + pltpu.roll(ext, 1, 0)
    s4 = s2 + pltpu.roll(s2, 2, 0)
    s8 = s4 + pltpu.roll(s4, 4, 0)
    s16 = s8 + pltpu.roll(s8, 8, 0)
    lane = lax.broadcasted_iota(i32, (tp, POOL_W), 1)
    grp = lane // POOL_GC
    sw = jnp.where(grp == 0, s2[H:], jnp.where(grp == 1, s4[H:], jnp.where(grp == 2, s8[H:], s16[H:])))
    win = jnp.where(grp == 0, POOL_WINDOWS[0],
                    jnp.where(grp == 1, POOL_WINDOWS[1],
                              jnp.where(grp == 2, POOL_WINDOWS[2], POOL_WINDOWS[3])))
    t = pl.program_id(1) * tp + lax.broadcasted_iota(i32, (tp, POOL_W), 0)
    cnt = jnp.minimum(t + 1, win).astype(f32)
    pooled = sw / cnt - u
    y = _dot(pooled.astype(bf16), w_ref[...]) * sc_ref[...]
    return (y * _silu(pg_ref[...])).astype(bf16)


def _mix_kernel(x_ref, yg_ref, yd_ref, pu_ref, pg_ref, wg_ref, wd_ref, wp_ref, wbd_ref, sc_ref, g_ref,
                *rest, project):
    halo_ref = rest[-1]
    yp = _pool_mixer(pu_ref, pg_ref, wbd_ref, sc_ref, halo_ref)
    x = x_ref[...] + _dot(yg_ref[...], wg_ref[...]) + _dot(yd_ref[...], wd_ref[...]) + _dot(yp, wp_ref[...])
    if project:
        wn_ref, wt_ref, xo_ref = rest[:3]
        xo_ref[...] = x
        _project(x, g_ref, wn_ref, wt_ref, rest[3:-1])
    else:
        rest[0][...] = x * lax.rsqrt(jnp.mean(x * x, axis=-1, keepdims=True) + EPS) * g_ref[...]


def _mix(x, yg, yd, pu, pg, wg, wd, wp, wbd, sc, g, wn=None, wt=None, *, tm):
    B, T, D = x.shape
    project = wn is not None
    tok = lambda w: pl.BlockSpec((None, tm, w), lambda b, t: (b, t, 0))
    const = lambda s: pl.BlockSpec(s, lambda b, t: (0, 0), pipeline_mode=pl.Buffered(1))
    in_specs = [tok(D), tok(GLA_HP), tok(DSA_W), tok(POOL_W), tok(POOL_W),
                const((GLA_HP, D)), const((DSA_W, D)), const((POOL_W, D)), const((POOL_W, POOL_W)),
                const((1, POOL_W)), const((1, D))]
    args = [x, yg, yd, pu, pg, wg, wd, wp, wbd, sc, g]
    out_shape, out_specs = [jax.ShapeDtypeStruct((B, T, D), f32)], [tok(D)]
    if project:
        in_specs += [const((D, NAT_COLS)), const((TR_ROWS, D))]
        args += [wn, wt]
        proj_shape, proj_specs = _proj_outs(B, T, tm)
        out_shape += proj_shape
        out_specs += proj_specs
    out = pl.pallas_call(
        functools.partial(_mix_kernel, project=project),
        grid=(B, T // tm),
        in_specs=in_specs,
        out_specs=out_specs,
        out_shape=out_shape,
        scratch_shapes=[pltpu.VMEM((POOL_HALO, POOL_W), f32)],
        compiler_params=_params(),
        name="mix_project" if project else "mix_final",
    )(*args)
    return (out[0], out[1:]) if project else out[0]


def _dsa_kernel(qT_ref, qiT_ref, wiT_ref, k_ref, kx_ref, vT_ref, dg_ref, brow_ref, bmax_ref, out_ref,
                keys_ref, planes_ref, active_ref, qm_ref, qim_ref, acc_ref, kmax_ref, bias_ref, *, topk):
    tq = qT_ref.shape[1]
    tk = tq
    qb = pl.program_id(1)
    n_blocks = qb + 1

    def rows(kb):
        return pl.ds(pl.multiple_of(kb * tk, tk), tk)

    @pl.when(qb == 0)
    def _():
        def body(kb, mx):
            kk = k_ref[rows(kb), :].astype(f32)
            n2 = jnp.sum(kk * kk, axis=1, keepdims=True)
            return jnp.maximum(mx, jnp.max(n2, axis=0, keepdims=True))
        mx = lax.fori_loop(0, k_ref.shape[0] // tk, body, jnp.zeros((1, 1), f32))
        kmax_ref[...] = jnp.broadcast_to(jnp.sqrt(mx), kmax_ref.shape)

    prow = lax.broadcasted_iota(i32, (LANE, tq), 0)
    for h in range(DSA_HEADS):
        pair = qT_ref[(h // 2) * LANE:(h // 2 + 1) * LANE, :]
        qm_ref[h // 2, :, (h % 2) * tq:(h % 2 + 1) * tq] = jnp.where(
            (prow // DSA_DH) == (h % 2), pair, jnp.zeros_like(pair))
    qi = qiT_ref[...]
    for h in range(IDX_HEADS):
        qim_ref[:, h * tq:(h + 1) * tq] = jnp.where((prow // IDX_DIM) == h, qi, jnp.zeros_like(qi))
    w = wiT_ref[...] * ((IDX_DIM ** -0.5) * (IDX_HEADS ** -0.5))

    def score_keys(r):
        d = _dot(kx_ref[r, :], qim_ref[...])
        s = jnp.zeros((d.shape[0], tq), f32)
        for h in range(IDX_HEADS):
            s = s + jnp.maximum(d[:, h * tq:(h + 1) * tq], 0.0) * w[h:h + 1, :]
        bits = lax.bitcast_convert_type(s, i32)
        return jnp.where(bits < 0, INT_MIN - bits, bits)

    def store_keys(kb, keys):
        keys_ref[rows(kb), :] = keys
        u = keys ^ INT_MIN
        words = [u[SUBLANE * k:SUBLANE * (k + 1), :] for k in range(32)]
        j, m = 16, 0x0000FFFF
        while j:
            for k in range(32):
                if k & j == 0:
                    t = (lax.shift_right_logical(words[k], jnp.int32(j)) ^ words[k + j]) & m
                    words[k + j] = words[k + j] ^ t
                    words[k] = words[k] ^ lax.shift_left(t, jnp.int32(j))
            j >>= 1
            m ^= m << j
        r8 = pl.ds(pl.multiple_of(kb * SUBLANE, SUBLANE), SUBLANE)
        for b in range(32):
            planes_ref[b, r8, :] = words[b]

    def score_body(i, c, span):
        keys = score_keys(pl.ds(pl.multiple_of(i * span, span), span))
        for part in range(span // tk):
            store_keys(i * (span // tk) + part, keys[part * tk:(part + 1) * tk, :])
        return c

    @pl.when((pl.program_id(0) == 0) & (qb == 0))
    def _():
        planes_ref[...] = jnp.zeros_like(planes_ref)
        for h in range(DSA_HEADS):
            for c in range(bias_ref.shape[1] // tk):
                x = jnp.broadcast_to(brow_ref[h:h + 1, :], (tk, brow_ref.shape[1]))
                x = pltpu.roll(x, c * tk, 1, stride=1, stride_axis=0)
                bias_ref[h, c * tk:(c + 1) * tk, :] = x[:, :tq]

    lax.fori_loop(0, qb // 4, functools.partial(score_body, span=4 * tk), 0)
    lax.fori_loop(2 * (qb // 4), qb // 2, functools.partial(score_body, span=2 * tk), 0)
    lax.fori_loop(2 * (qb // 2), qb, functools.partial(score_body, span=tk), 0)
    krow = lax.broadcasted_iota(i32, (tk, tq), 0)
    qcol = lax.broadcasted_iota(i32, (tk, tq), 1)
    store_keys(qb, jnp.where(krow <= qcol, score_keys(rows(qb)), INT_MIN))

    for extra in range(MASK_GROUP - 1):
        keys_ref[rows(n_blocks + extra), :] = jnp.full((tk, tq), INT_MIN, i32)

    n_rows = planes_ref.shape[1]
    prow8 = lax.broadcasted_iota(i32, (n_rows, tq), 0)
    active_ref[...] = jnp.where(prow8 < n_blocks * SUBLANE, -1, 0)

    half = n_rows // 2
    upper = n_blocks * SUBLANE > half

    def count_set(b, r0):
        ones = lax.population_count(active_ref[r0:r0 + half, :] & planes_ref[b, r0:r0 + half, :])
        return jnp.sum(jnp.sum(ones.reshape(half // SUBLANE, SUBLANE, tq), axis=0), axis=0, keepdims=True)

    def narrow(b, r0, flip):
        active_ref[r0:r0 + half, :] = active_ref[r0:r0 + half, :] & (planes_ref[b, r0:r0 + half, :] ^ flip)

    def radix_step(i, carry):
        n_gt, tau_u = carry
        b = 31 - i
        n_set = count_set(b, 0) + lax.cond(upper, lambda: count_set(b, half),
                                           lambda: jnp.zeros((1, tq), i32))
        take = n_gt + n_set >= topk
        flip = jnp.where(take, 0, -1)
        narrow(b, 0, flip)

        @pl.when(upper)
        def _():
            narrow(b, half, flip)

        return (jnp.where(take, n_gt, n_gt + n_set),
                tau_u | jnp.where(take, lax.shift_left(jnp.int32(1), b), 0))

    n_gt, tau_u = lax.fori_loop(0, 32, radix_step, (jnp.zeros((1, tq), i32), jnp.zeros((1, tq), i32)))
    tau = tau_u ^ INT_MIN
    need = jnp.where(tau == INT_MIN, 0, topk - n_gt).astype(f32)

    tri = (lax.broadcasted_iota(i32, (tk, tk), 1) <= lax.broadcasted_iota(i32, (tk, tk), 0)).astype(bf16)

    def mask_body(g, run):
        ks = [keys_ref[rows(MASK_GROUP * g + part), :] for part in range(MASK_GROUP)]
        eqs = [k == tau for k in ks]
        cnt = [_dot(tri, e.astype(bf16)) for e in eqs]
        for part in range(MASK_GROUP):
            pre = cnt[part] + run
            sel = (ks[part] > tau) | (eqs[part] & (pre <= need))
            keys_ref[rows(MASK_GROUP * g + part), :] = lax.bitcast_convert_type(
                jnp.where(sel, 0.0, NEG_BIG).astype(f32), i32)
            run = pre[tk - 1:tk, :]
        return run

    lax.fori_loop(0, (n_blocks + MASK_GROUP - 1) // MASK_GROUP, mask_body, jnp.zeros((1, tq), f32))

    def tile_ops(kb, span, near):
        r = pl.ds(pl.multiple_of(kb * tk, tk), span)
        ones = jnp.ones((ONES_ROWS, span), bf16)
        madd = lax.bitcast_convert_type(keys_ref[r, :], f32)
        dots = [_dot(k_ref[r, g * LANE:(g + 1) * LANE], qm_ref[g]) for g in range(DSA_HEADS // 2)]

        def logits(h):
            lg = dots[h // 2][:, (h % 2) * tq:(h % 2 + 1) * tq] + madd
            if near:
                j = kb - (qb - 2)
                lg = lg + bias_ref[h, pl.ds(pl.multiple_of(j * tk, tk), span), :]
            return lg

        def v1(h):
            return jnp.concatenate([vT_ref[h * DSA_DH:(h + 1) * DSA_DH, r], ones], axis=0)

        return logits, v1

    def attend_online(kb, ms, span, near):
        logits, v1 = tile_ops(kb, span, near)
        new_ms = []
        for h in range(DSA_HEADS):
            lg = logits(h)
            m_new = jnp.maximum(ms[h], jnp.max(lg, axis=0, keepdims=True))
            alpha = jnp.exp2(ms[h] - m_new)
            acc_ref[h] = alpha * acc_ref[h] + _dot(v1(h), jnp.exp2(lg - m_new).astype(bf16))
            new_ms.append(m_new)
        return tuple(new_ms)

    def attend_fixed(kb, shift, span, near):
        logits, v1 = tile_ops(kb, span, near)
        for h in range(DSA_HEADS):
            acc_ref[h] = acc_ref[h] + _dot(v1(h), jnp.exp2(logits(h) - shift[h]).astype(bf16))
        return shift

    def sweep(body, carry):
        n_far = jnp.maximum(qb - 1, 0)
        n_far_pairs = n_far // 2
        n_far_quads = n_far_pairs // 2
        has_prev = jnp.minimum(qb, 1)
        acc_ref[...] = jnp.zeros_like(acc_ref)
        carry = lax.fori_loop(0, n_far_quads, lambda i, c: body(4 * i, c, span=4 * tk, near=False), carry)
        carry = lax.fori_loop(2 * n_far_quads, n_far_pairs,
                              lambda i, c: body(2 * i, c, span=2 * tk, near=False), carry)
        carry = lax.fori_loop(2 * n_far_pairs, n_far, lambda i, c: body(i, c, span=tk, near=False), carry)
        carry = lax.fori_loop(0, has_prev, lambda i, c: body(qb - 1, c, span=2 * tk, near=True), carry)
        return lax.fori_loop(0, 1 - has_prev, lambda i, c: body(qb, c, span=tk, near=True), carry)

    q_norm = []
    for g in range(DSA_HEADS // 2):
        qf = qm_ref[g].astype(f32)
        n2 = jnp.sum(qf * qf, axis=0, keepdims=True)
        q_norm += [jnp.sqrt(n2[:, :tq]), jnp.sqrt(n2[:, tq:])]
    shift = tuple(q_norm[h] * (kmax_ref[0:1, 0:1] * SHIFT_MARGIN) + bmax_ref[h:h + 1, 0:1]
                  for h in range(DSA_HEADS))
    sweep(attend_fixed, shift)
    denom = acc_ref[0, DSA_DH:DSA_DH + 1, :]
    for h in range(1, DSA_HEADS):
        denom = jnp.minimum(denom, acc_ref[h, DSA_DH:DSA_DH + 1, :])
    accurate = jnp.min(denom) >= MIN_DENOM

    @pl.when(jnp.logical_not(accurate))
    def _():
        sweep(attend_online, tuple(jnp.full((1, tq), NEG_BIG, f32) for _ in range(DSA_HEADS)))

    oT = jnp.concatenate([acc_ref[h, :DSA_DH, :] / acc_ref[h, DSA_DH:DSA_DH + 1, :]
                          for h in range(DSA_HEADS)], axis=0)
    out_ref[...] = (oT.T * _silu(dg_ref[...])).astype(out_ref.dtype)


def _dsa(dqT, qiT, wiT, dk, kx, dvT, dg, bias, bmax, tq, topk):
    B, T, _ = dk.shape
    whole = lambda shape, imap: pl.BlockSpec(shape, imap, pipeline_mode=pl.Buffered(1))
    return pl.pallas_call(
        functools.partial(_dsa_kernel, topk=topk),
        grid=(B, T // tq),
        in_specs=[pl.BlockSpec((None, DSA_W, tq), lambda b, q: (b, 0, q)),
                  pl.BlockSpec((None, IDX_HEADS * IDX_DIM, tq), lambda b, q: (b, 0, q)),
                  pl.BlockSpec((None, SUBLANE, tq), lambda b, q: (b, 0, q)),
                  whole((None, T, DSA_W), lambda b, q: (b, 0, 0)),
                  whole((None, T, LANE), lambda b, q: (b, 0, 0)),
                  whole((None, DSA_W, T), lambda b, q: (b, 0, 0)),
                  pl.BlockSpec((None, tq, DSA_W), lambda b, q: (b, q, 0)),
                  pl.BlockSpec((SUBLANE, 4 * tq), lambda b, q: (0, 0)),
                  pl.BlockSpec((SUBLANE, LANE), lambda b, q: (0, 0))],
        out_specs=pl.BlockSpec((None, tq, DSA_W), lambda b, q: (b, q, 0)),
        out_shape=jax.ShapeDtypeStruct((B, T, DSA_W), bf16),
        scratch_shapes=[pltpu.VMEM((T + (MASK_GROUP - 1) * tq, tq), i32),
                        pltpu.VMEM((32, T // 32, tq), i32),
                        pltpu.VMEM((T // 32, tq), i32),
                        pltpu.VMEM((DSA_HEADS // 2, LANE, 2 * tq), bf16),
                        pltpu.VMEM((LANE, IDX_HEADS * tq), bf16),
                        pltpu.VMEM((DSA_HEADS, DSA_DH + ONES_ROWS, tq), f32),
                        pltpu.VMEM((SUBLANE, LANE), f32),
                        pltpu.VMEM((DSA_HEADS, 3 * tq, tq), f32)],
        compiler_params=_params(),
        name="dsa",
    )(dqT, qiT, wiT, dk, kx, dvT, dg, bias, bmax)


def _pad_heads(w, heads, d, slot=LANE):
    lead = w.shape[:-1]
    w = w.reshape(*lead, heads, d)
    w = jnp.pad(w, [(0, 0)] * len(lead) + [(0, 0), (0, slot - d)])
    return w.reshape(*lead, heads * slot)


def _split_cols(w):
    outs, off = [], 0
    for s in IN_SIZES:
        outs.append(w[..., off:off + s])
        off += s
    return outs


def _layer_weights(w_in, gate_w2, gate_b, gla_norm_g, pool_w, pool_scale, w_out):
    (gq, gk, gv, gz, gg, dq, dk, dv, dg, qi, ki, wi, pu, pg) = _split_cols(w_in)
    D = w_in.shape[0]
    gz_lanes = slice(GLA_DK, GLA_DK + GLA_GATE_RANK)
    gq_slab = _pad_heads(gq, GLA_HEADS, GLA_DK, GLA_QK_SLOT).at[:, gz_lanes].set(gz)
    wn = jnp.concatenate([
        gq_slab, _pad_heads(gk, GLA_HEADS, GLA_DK, GLA_QK_SLOT),
        _pad_heads(gv, GLA_HEADS, GLA_DV), _pad_heads(gg, GLA_HEADS, GLA_DV),
        dk, jnp.tile(ki, (1, IDX_HEADS)), dg, pu, pg], axis=1).astype(bf16)
    wt = jnp.concatenate([
        dq, dv, qi, jnp.pad(wi, ((0, 0), (0, SUBLANE - IDX_HEADS)))], axis=1).T.astype(bf16)
    w2 = jnp.zeros((GLA_QK_W, GLA_QK_W), f32).at[gz_lanes, :].set(
        _pad_heads(gate_w2, GLA_HEADS, GLA_DK, GLA_QK_SLOT)).astype(bf16)
    b2 = _pad_heads(gate_b[None, :], GLA_HEADS, GLA_DK, GLA_QK_SLOT)
    gn = jnp.pad(gla_norm_g[None, :], ((0, 0), (0, LANE - GLA_DV)))
    wbd = jax.scipy.linalg.block_diag(*[pool_w[g] for g in range(POOL_GROUPS)]).astype(bf16)
    sc = pool_scale[None, :]
    wg = _pad_heads(w_out[:GLA_W].T, GLA_HEADS, GLA_DV).T.astype(bf16)
    wd = w_out[GLA_W:GLA_W + DSA_W].astype(bf16)
    wp = w_out[GLA_W + DSA_W:].astype(bf16)
    return wn, wt, w2, b2, gn, wbd, sc, wg, wd, wp


def _t5_bucket_table(n):
    rel = np.arange(n)
    max_exact = REL_BUCKETS // 2
    relf = np.maximum(rel, 1).astype(np.float32)
    large = max_exact + (np.log(relf / np.float32(max_exact)) / np.float32(math.log(REL_MAX_DIST / max_exact))
                         * np.float32(REL_BUCKETS - max_exact)).astype(np.int32)
    large = np.minimum(large, REL_BUCKETS - 1)
    return np.where(rel < max_exact, rel, large)


def _near_bias(rel_bias, tq):
    bucket = _t5_bucket_table(3 * tq)
    assert np.all(bucket[tq + 1:] == REL_BUCKETS - 1)
    by_rel = (rel_bias[bucket] - rel_bias[REL_BUCKETS - 1][None, :]) * LOG2E
    n = 4 * tq
    u = np.arange(n)
    u = np.where(u < 2 * tq, u, u - n)
    brow = jnp.pad(by_rel[np.maximum(2 * tq + u, 0)].T, ((0, SUBLANE - DSA_HEADS), (0, 0)))
    bmax = jnp.max(by_rel, axis=0)
    bmax = jnp.broadcast_to(jnp.pad(bmax, (0, SUBLANE - DSA_HEADS))[:, None], (SUBLANE, LANE))
    return brow, bmax


def kernel(x, norm_g, w_in, gla_gate_w2, gla_gate_b, gla_norm_g, rel_bias, pool_w, pool_scale, w_out,
           final_norm_g):
    B, T, D = x.shape
    assert D == D_MODEL
    tm = min(512, T)
    tq = min(256, T)
    topk = min(TOPK_MAX, T // 4)
    assert T % tm == 0 and T % (2 * tq) == 0 and tq >= LANE and T % GLA_CHUNK == 0
    bias, bmax = _near_bias(rel_bias.astype(f32), tq)
    weights = [_layer_weights(w_in[l], gla_gate_w2[l], gla_gate_b[l], gla_norm_g[l], pool_w[l],
                              pool_scale[l], w_out[l]) for l in range(DEPTH)]
    proj = _in_proj(x, norm_g[0][None, :], weights[0][0], weights[0][1], tm)
    for l in range(DEPTH):
        _, _, w2, b2, gn, wbd, sc, wg, wd, wp = weights[l]
        (gq, gk, gv, gg, dk, kx, dg, pu, pg, dqT, dvT, qiT, wiT) = proj
        yg = _gla(gq, gk, gv, gg, w2, b2, gn, tm)
        yd = _dsa(dqT, qiT, wiT, dk, kx, dvT, dg, bias, bmax, tq, topk)
        if l + 1 < DEPTH:
            x, proj = _mix(x, yg, yd, pu, pg, wg, wd, wp, wbd, sc, norm_g[l + 1][None, :],
                           weights[l + 1][0], weights[l + 1][1], tm=tm)
        else:
            x = _mix(x, yg, yd, pu, pg, wg, wd, wp, wbd, sc, final_norm_g[None, :], tm=tm)
    return x
```
